```python
import jax, jax.numpy as jnp
from jax import lax
import numpy as np

D_MODEL = 1024
BATCH = 4
SEQ = 4096
DEPTH = 4
DEC_BATCH = 128
DEC_SEQ = 8
PAST_LEN = 8192
PAGE_SIZE = 128

HEAD_DIM = 64
ATTN_SCALE = HEAD_DIM ** -0.5
N_HEADS_A = 8
A_PATTERNS = ((128, 1), (512, 4), (2048, 16))
A_MAX_WINDOW = 2048
N_HEADS_B = 8
N_KV_B = 2
G_B = N_HEADS_B // N_KV_B
WINDOW_B = 128
N_HEADS_C = 16
N_KV_C = 4
G_C = N_HEADS_C // N_KV_C
FOX_BLOCK = 128
FORGET_BIAS_INIT = 3.0
D_FF = 2816
PLE_DIM = 256
ROPE_THETA = 10000.0
RMS_EPS = 1e-6
N_EVEN = (DEPTH + 1) // 2
N_ODD = DEPTH // 2

A_WIDTH = N_HEADS_A * HEAD_DIM
B_Q_WIDTH = N_HEADS_B * HEAD_DIM
B_KV_WIDTH = N_KV_B * HEAD_DIM
C_Q_WIDTH = N_HEADS_C * HEAD_DIM
C_KV_WIDTH = N_KV_C * HEAD_DIM
AB_IN_WIDTH = 3 * A_WIDTH + B_Q_WIDTH + 2 * B_KV_WIDTH
AB_OUT_WIDTH = A_WIDTH + B_Q_WIDTH
AB_SPLITS = (A_WIDTH, 2 * A_WIDTH, 3 * A_WIDTH, 3 * A_WIDTH + B_Q_WIDTH, 3 * A_WIDTH + B_Q_WIDTH + B_KV_WIDTH)
C_IN_WIDTH = C_Q_WIDTH + 2 * C_KV_WIDTH + N_HEADS_C
C_SPLITS = (C_Q_WIDTH, C_Q_WIDTH + C_KV_WIDTH, C_Q_WIDTH + 2 * C_KV_WIDTH)

kernel_name = "hybrid_dilated_swa_fox_decoder_step"

F32 = jnp.float32


def rms_norm(x, g):
    xf = x.astype(F32)
    y = xf * lax.rsqrt(jnp.mean(xf * xf, axis=-1, keepdims=True) + RMS_EPS)
    return (y * g.astype(F32)).astype(x.dtype)


def half_swiglu(x, g, w_in, w_out):
    a, b = jnp.split(rms_norm(x, g) @ w_in, 2, axis=-1)
    return x + 0.5 * ((jax.nn.silu(a) * b) @ w_out)


def ple_add(x, p, g, w_gate, w_proj):
    return x + jax.nn.sigmoid(rms_norm(x, g) @ w_gate) * (p.astype(x.dtype) @ w_proj)


def rope(x, pos):
    half = x.shape[-1] // 2
    inv_freq = ROPE_THETA ** (-jnp.arange(half, dtype=F32) / half)
    ang = pos.astype(F32)[:, None] * inv_freq[None, :]
    cos, sin = jnp.cos(ang)[:, None, :], jnp.sin(ang)[:, None, :]
    xf = x.astype(F32)
    x1, x2 = xf[..., :half], xf[..., half:]
    return jnp.concatenate([x1 * cos - x2 * sin, x2 * cos + x1 * sin], axis=-1).astype(x.dtype)


def banded_stats(q, k, v, dilation, span):
    B, S = q.shape[:2]
    L = S // dilation
    nb = -(-L // span)
    Lp = nb * span

    def to_blocks(x):
        x = x.reshape((B, L, dilation) + x.shape[2:])
        x = jnp.pad(x, [(0, 0), (0, Lp - L)] + [(0, 0)] * (x.ndim - 2))
        return x.reshape((B, nb, span, dilation) + x.shape[3:])

    def with_prev(x):
        prev = jnp.pad(x, [(0, 0), (1, 0)] + [(0, 0)] * (x.ndim - 2))[:, :-1]
        return jnp.concatenate([prev, x], axis=2)

    def from_blocks(x):
        x = x.reshape((B, Lp, dilation) + x.shape[4:])[:, :L]
        return x.reshape((B, S) + x.shape[3:])

    qb = to_blocks(q * ATTN_SCALE)
    kb, vb = with_prev(to_blocks(k)), with_prev(to_blocks(v))
    s = jnp.einsum("bnqrhgd,bnkrhd->bnrhgqk", qb, kb, preferred_element_type=F32)
    qi = jnp.arange(span)[:, None]
    ki = jnp.arange(2 * span)[None, :]
    rel = span + qi - ki
    band = (rel >= 0) & (rel <= span)
    has_prev = (jnp.arange(nb) > 0)[:, None, None] | (ki >= span)[None]
    valid = band[None] & has_prev
    s = jnp.where(valid[None, :, None, None, None], s, -jnp.inf)
    m = jnp.max(s, axis=-1)
    e = jnp.exp(s - m[..., None])
    den = jnp.sum(e, axis=-1)
    num = jnp.einsum("bnrhgqk,bnkrhd->bnqrhgd", e.astype(v.dtype), vb, preferred_element_type=F32)
    stat = lambda x: from_blocks(jnp.moveaxis(x, -1, 2))
    return from_blocks(num), stat(den), stat(m)


def dilated_offsets(buf, n_new, dilation, span):
    idx = buf + jnp.arange(n_new)[:, None] - dilation * jnp.arange(span + 1)[None, :]
    return jnp.maximum(idx, 0), idx >= 0


def gathered_stats(q, k_all, v_all, idx, valid):
    kg = jnp.take(k_all, idx, axis=1)
    vg = jnp.take(v_all, idx, axis=1)
    s = jnp.einsum("bthgd,btnhd->bthgn", q * ATTN_SCALE, kg, preferred_element_type=F32)
    s = jnp.where(valid[None, :, None, None, :], s, -jnp.inf)
    m = jnp.max(s, axis=-1)
    e = jnp.exp(s - m[..., None])
    num = jnp.einsum("bthgn,btnhd->bthgd", e.astype(vg.dtype), vg, preferred_element_type=F32)
    return num, jnp.sum(e, axis=-1), m


def mix_by_denominator(stats):
    big_m = stats[0][2]
    for _, _, m in stats[1:]:
        big_m = jnp.maximum(big_m, m)
    num = sum(jnp.exp(m - big_m)[..., None] * n for n, _, m in stats)
    den = sum(jnp.exp(m - big_m) * d for _, d, m in stats)
    return num / den[..., None]


def with_sink(stats, sink):
    num, den, m = stats
    sink = sink.astype(F32).reshape(m.shape[-2:])
    big_m = jnp.maximum(m, sink)
    a = jnp.exp(m - big_m)
    return num * a[..., None] / (den * a + jnp.exp(sink - big_m))[..., None]


def project_ab(h, w_in, pos):
    B, S, _ = h.shape
    qa, ka, va, qb, kb, vb = jnp.split(h @ w_in, AB_SPLITS, axis=-1)
    heads = lambda x, n: x.reshape(B, S, n, HEAD_DIM)
    qa = rope(heads(qa, N_HEADS_A), pos)[:, :, :, None]
    ka = rope(heads(ka, N_HEADS_A), pos)
    va = heads(va, N_HEADS_A)
    qb = rope(heads(qb, N_HEADS_B), pos).reshape(B, S, N_KV_B, G_B, HEAD_DIM)
    kb = rope(heads(kb, N_KV_B), pos)
    vb = heads(vb, N_KV_B)
    return qa, ka, va, qb, kb, vb


def merge_ab(oa, ob, dtype):
    B, S = oa.shape[:2]
    return jnp.concatenate([oa.reshape(B, S, A_WIDTH), ob.reshape(B, S, B_Q_WIDTH)], axis=-1).astype(dtype)


def mixer_ab_prompt(h, w_in, w_out, sink):
    B, S, _ = h.shape
    qa, ka, va, qb, kb, vb = project_ab(h, w_in, jnp.arange(S))
    oa = mix_by_denominator([banded_stats(qa, ka, va, r, w // r) for w, r in A_PATTERNS])
    ob = with_sink(banded_stats(qb, kb, vb, 1, WINDOW_B), sink)
    y = merge_ab(oa, ob, h.dtype) @ w_out
    na, nw = min(A_MAX_WINDOW, S), min(WINDOW_B, S)
    return y, ka[:, S - na:], va[:, S - na:], kb[:, S - nw:], vb[:, S - nw:]


def mixer_ab_sample(h, ck_a, cv_a, ck_b, cv_b, w_in, w_out, sink):
    B, T, _ = h.shape
    qa, ka, va, qb, kb, vb = project_ab(h, w_in, PAST_LEN + jnp.arange(T))
    ka_all = jnp.concatenate([ck_a, ka], axis=1)
    va_all = jnp.concatenate([cv_a, va], axis=1)
    kb_all = jnp.concatenate([ck_b, kb], axis=1)
    vb_all = jnp.concatenate([cv_b, vb], axis=1)
    stats_a = []
    for w, r in A_PATTERNS:
        idx, valid = dilated_offsets(ck_a.shape[1], T, r, w // r)
        stats_a.append(gathered_stats(qa, ka_all, va_all, idx, valid))
    oa = mix_by_denominator(stats_a)
    idx, valid = dilated_offsets(ck_b.shape[1], T, 1, WINDOW_B)
    ob = with_sink(gathered_stats(qb, kb_all, vb_all, idx, valid), sink)
    y = merge_ab(oa, ob, h.dtype) @ w_out
    return y, ka_all[:, T:], va_all[:, T:], kb_all[:, T:], vb_all[:, T:]


def project_c(h, w_in, b_forget):
    B, S, _ = h.shape
    q, k, v, f = jnp.split(h @ w_in, C_SPLITS, axis=-1)
    q = q.reshape(B, S, N_KV_C, G_C, HEAD_DIM)
    k = k.reshape(B, S, N_KV_C, HEAD_DIM)
    v = v.reshape(B, S, N_KV_C, HEAD_DIM)
    logf = jax.nn.log_sigmoid(f.astype(F32) + b_forget.astype(F32))
    return q, k, v, logf


def mixer_c_prompt(h, w_in, b_forget, w_out):
    B, S, _ = h.shape
    q, k, v, logf = project_c(h, w_in, b_forget)
    cum = jnp.cumsum(logf, axis=1).reshape(B, S, N_KV_C, G_C)
    nblk = S // FOX_BLOCK
    qb = jnp.swapaxes(q.reshape(B, nblk, FOX_BLOCK, N_KV_C, G_C, HEAD_DIM), 0, 1)
    cb = jnp.swapaxes(cum.reshape(B, nblk, FOX_BLOCK, N_KV_C, G_C), 0, 1)
    cum_k = jnp.moveaxis(cum, 1, -1)
    kpos = jnp.arange(S)

    def block(args):
        qn, cn, n = args
        qpos = n * FOX_BLOCK + jnp.arange(FOX_BLOCK)
        s = jnp.einsum("bqhgd,bkhd->bhgqk", qn * ATTN_SCALE, k, preferred_element_type=F32)
        s = s + jnp.moveaxis(cn, 1, -1)[..., :, None] - cum_k[..., None, :]
        s = jnp.where(kpos[None, :] <= qpos[:, None], s, -jnp.inf)
        p = jax.nn.softmax(s, axis=-1)
        return jnp.einsum("bhgqk,bkhd->bqhgd", p.astype(v.dtype), v, preferred_element_type=F32)

    o = jnp.swapaxes(lax.map(block, (qb, cb, jnp.arange(nblk))), 0, 1).reshape(B, S, C_Q_WIDTH)
    return o.astype(h.dtype) @ w_out, k, v, logf


def mixer_c_sample(h, ck, cv, cl, page_table, w_in, b_forget, w_out):
    B, T, _ = h.shape
    q, k, v, logf = project_c(h, w_in, b_forget)
    gather = lambda c: c[page_table].reshape((B, -1) + c.shape[2:])
    kp, vp, lp = gather(ck), gather(cv), gather(cl)
    P = kp.shape[1]
    cum = jnp.cumsum(jnp.concatenate([lp.astype(F32), logf], axis=1), axis=1)
    cum_k = jnp.moveaxis(cum.reshape(B, P + T, N_KV_C, G_C), 1, -1)
    cum_q = cum_k[..., P:]
    qs = q * ATTN_SCALE
    s_past = jnp.einsum("bthgd,bkhd->bhgtk", qs, kp, preferred_element_type=F32)
    s_past = s_past + cum_q[..., :, None] - cum_k[..., None, :P]
    s_new = jnp.einsum("bthgd,bkhd->bhgtk", qs, k, preferred_element_type=F32)
    s_new = s_new + cum_q[..., :, None] - cum_q[..., None, :]
    s_new = jnp.where(jnp.arange(T)[None, :] <= jnp.arange(T)[:, None], s_new, -jnp.inf)
    p = jax.nn.softmax(jnp.concatenate([s_past, s_new], axis=-1), axis=-1)
    o = (jnp.einsum("bhgtk,bkhd->bthgd", p[..., :P].astype(vp.dtype), vp, preferred_element_type=F32)
         + jnp.einsum("bhgtk,bkhd->bthgd", p[..., P:].astype(v.dtype), v, preferred_element_type=F32))
    return o.reshape(B, T, C_Q_WIDTH).astype(h.dtype) @ w_out, k, v, logf


def setup_inputs(seed: int = 0) -> dict:
    key = jax.random.key(seed)
    keys = list(jax.random.split(key, 32))

    def normal(shape, scale=1.0):
        return scale * jax.random.normal(keys.pop(), shape, F32)

    def gain(shape):
        return 1.0 + 0.05 * normal(shape)

    n_pages = PAST_LEN // PAGE_SIZE
    n_used = DEC_BATCH * n_pages
    n_pool = n_used + n_used // 4
    buf_a = min(A_MAX_WINDOW, PAST_LEN)
    buf_b = min(WINDOW_B, PAST_LEN)
    return {
        "x_prompt": normal((BATCH, SEQ, D_MODEL)),
        "x_sample": normal((DEC_BATCH, DEC_SEQ, D_MODEL)),
        "cache_a_k": normal((N_EVEN, DEC_BATCH, buf_a, N_HEADS_A, HEAD_DIM)),
        "cache_a_v": normal((N_EVEN, DEC_BATCH, buf_a, N_HEADS_A, HEAD_DIM)),
        "cache_b_k": normal((N_EVEN, DEC_BATCH, buf_b, N_KV_B, HEAD_DIM)),
        "cache_b_v": normal((N_EVEN, DEC_BATCH, buf_b, N_KV_B, HEAD_DIM)),
        "cache_c_k": normal((N_ODD, n_pool, PAGE_SIZE, N_KV_C, HEAD_DIM)),
        "cache_c_v": normal((N_ODD, n_pool, PAGE_SIZE, N_KV_C, HEAD_DIM)),
        "cache_c_logf": jax.nn.log_sigmoid(FORGET_BIAS_INIT + normal((N_ODD, n_pool, PAGE_SIZE, N_HEADS_C))),
        "page_table": jax.random.permutation(keys.pop(), n_pool)[:n_used].reshape(DEC_BATCH, n_pages).astype(jnp.int32),
        "p_prompt": normal((DEPTH, BATCH, SEQ, PLE_DIM)),
        "p_sample": normal((DEPTH, DEC_BATCH, DEC_SEQ, PLE_DIM)),
        "g_ffn1": gain((DEPTH, D_MODEL)),
        "w_ffn1_in": normal((DEPTH, D_MODEL, 2 * D_FF), D_MODEL ** -0.5),
        "w_ffn1_out": normal((DEPTH, D_FF, D_MODEL), D_FF ** -0.5),
        "g_mix": gain((DEPTH, D_MODEL)),
        "w_in_ab": normal((N_EVEN, D_MODEL, AB_IN_WIDTH), D_MODEL ** -0.5),
        "w_out_ab": normal((N_EVEN, AB_OUT_WIDTH, D_MODEL), AB_OUT_WIDTH ** -0.5),
        "sink_b": normal((N_EVEN, N_HEADS_B), 0.5),
        "w_in_c": normal((N_ODD, D_MODEL, C_IN_WIDTH), D_MODEL ** -0.5),
        "b_forget": FORGET_BIAS_INIT + normal((N_ODD, N_HEADS_C), 0.5),
        "w_out_c": normal((N_ODD, C_Q_WIDTH, D_MODEL), C_Q_WIDTH ** -0.5),
        "g_ffn2": gain((DEPTH, D_MODEL)),
        "w_ffn2_in": normal((DEPTH, D_MODEL, 2 * D_FF), D_MODEL ** -0.5),
        "w_ffn2_out": normal((DEPTH, D_FF, D_MODEL), D_FF ** -0.5),
        "g_ple": gain((DEPTH, D_MODEL)),
        "w_ple_gate": normal((DEPTH, D_MODEL, D_MODEL), D_MODEL ** -0.5),
        "w_ple_proj": normal((DEPTH, PLE_DIM, D_MODEL), PLE_DIM ** -0.5),
        "g_final": gain((D_MODEL,)),
    }


def reference(x_prompt, x_sample, cache_a_k, cache_a_v, cache_b_k, cache_b_v, cache_c_k, cache_c_v,
              cache_c_logf, page_table, p_prompt, p_sample, g_ffn1, w_ffn1_in, w_ffn1_out, g_mix,
              w_in_ab, w_out_ab, sink_b, w_in_c, b_forget, w_out_c, g_ffn2, w_ffn2_in, w_ffn2_out,
              g_ple, w_ple_gate, w_ple_proj, g_final):
    xp, xs = x_prompt, x_sample
    ak_p, av_p, ak_s, av_s = [], [], [], []
    bk_p, bv_p, bk_s, bv_s = [], [], [], []
    ck_p, cv_p, cl_p, ck_s, cv_s, cl_s = [], [], [], [], [], []
    for l in range(DEPTH):
        xp = half_swiglu(xp, g_ffn1[l], w_ffn1_in[l], w_ffn1_out[l])
        xs = half_swiglu(xs, g_ffn1[l], w_ffn1_in[l], w_ffn1_out[l])
        hp, hs = rms_norm(xp, g_mix[l]), rms_norm(xs, g_mix[l])
        if l % 2 == 0:
            e = l // 2
            yp, a_k, a_v, b_k, b_v = mixer_ab_prompt(hp, w_in_ab[e], w_out_ab[e], sink_b[e])
            ak_p.append(a_k); av_p.append(a_v); bk_p.append(b_k); bv_p.append(b_v)
            ys, a_k, a_v, b_k, b_v = mixer_ab_sample(hs, cache_a_k[e], cache_a_v[e], cache_b_k[e], cache_b_v[e],
                                                     w_in_ab[e], w_out_ab[e], sink_b[e])
            ak_s.append(a_k); av_s.append(a_v); bk_s.append(b_k); bv_s.append(b_v)
        else:
            o = l // 2
            yp, c_k, c_v, c_l = mixer_c_prompt(hp, w_in_c[o], b_forget[o], w_out_c[o])
            ck_p.append(c_k); cv_p.append(c_v); cl_p.append(c_l)
            ys, c_k, c_v, c_l = mixer_c_sample(hs, cache_c_k[o], cache_c_v[o], cache_c_logf[o], page_table,
                                               w_in_c[o], b_forget[o], w_out_c[o])
            ck_s.append(c_k); cv_s.append(c_v); cl_s.append(c_l)
        xp, xs = xp + yp, xs + ys
        xp = half_swiglu(xp, g_ffn2[l], w_ffn2_in[l], w_ffn2_out[l])
        xs = half_swiglu(xs, g_ffn2[l], w_ffn2_in[l], w_ffn2_out[l])
        xp = ple_add(xp, p_prompt[l], g_ple[l], w_ple_gate[l], w_ple_proj[l])
        xs = ple_add(xs, p_sample[l], g_ple[l], w_ple_gate[l], w_ple_proj[l])
    y_prompt = rms_norm(xp, g_final)
    y_sample = rms_norm(xs, g_final)
    return (y_prompt, y_sample,
            jnp.stack(ak_p), jnp.stack(av_p), jnp.stack(ak_s), jnp.stack(av_s),
            jnp.stack(bk_p), jnp.stack(bv_p), jnp.stack(bk_s), jnp.stack(bv_s),
            jnp.stack(ck_p), jnp.stack(cv_p), jnp.stack(cl_p),
            jnp.stack(ck_s), jnp.stack(cv_s), jnp.stack(cl_s))
```

```python
import functools

import jax
import jax.numpy as jnp
from jax import lax
from jax.experimental import pallas as pl
from jax.experimental.pallas import tpu as pltpu

F32 = jnp.float32
BF16 = jnp.bfloat16

HEAD_DIM = 64
HALF = HEAD_DIM // 2
ATTN_SCALE = HEAD_DIM ** -0.5
A_PATTERNS = ((128, 1), (512, 4), (2048, 16))
N_HEADS_A = 8
N_HEADS_B = 8
N_KV_B = 2
G_B = N_HEADS_B // N_KV_B
WINDOW_B = 128
N_HEADS_C = 16
N_KV_C = 4
G_C = N_HEADS_C // N_KV_C
ROPE_THETA = 10000.0
RMS_EPS = 1e-6
NEG = -1e30

A_W = N_HEADS_A * HEAD_DIM
BQ_W = N_HEADS_B * HEAD_DIM
BKV_W = N_KV_B * HEAD_DIM
CQ_W = N_HEADS_C * HEAD_DIM
CKV_W = N_KV_C * HEAD_DIM

VMEM_LIMIT = 56 * 1024 * 1024
LANES = 128
ROW_TILE = 512
FF_CHUNK = 256
BAND = 128
FOX_TQ = 256
FOX_TK = 512
CUM_BLOCK = 512
PAGES_PER_STEP = 8


def _dot(a, b):
    return jnp.dot(a, b, preferred_element_type=F32)


def _dot_nt(a, b):
    return lax.dot_general(a, b, (((1,), (1,)), ((), ())), preferred_element_type=F32)


def _dot_exact(a, b):
    return jnp.dot(a, b, preferred_element_type=F32, precision=lax.Precision.HIGHEST)


def _rms(x, g):
    return x * lax.rsqrt(jnp.mean(x * x, axis=-1, keepdims=True) + RMS_EPS) * g


def _params(*sem):
    return pltpu.CompilerParams(dimension_semantics=sem, vmem_limit_bytes=VMEM_LIMIT)


def _resident(shape, index_map):
    return pl.BlockSpec(shape, index_map, pipeline_mode=pl.Buffered(1))


def _ffn_kernel(x_ref, g_ref, win_ref, wout_ref, o_ref, act_ref):
    d_ff = wout_ref.shape[0]
    x = x_ref[...]
    h = _rms(x, g_ref[...]).astype(BF16)
    for c in range(d_ff // FF_CHUNK):
        lo = c * FF_CHUNK
        a = _dot(h, win_ref[:, lo:lo + FF_CHUNK])
        b = _dot(h, win_ref[:, d_ff + lo:d_ff + lo + FF_CHUNK])
        act_ref[:, lo:lo + FF_CHUNK] = (a * jax.nn.sigmoid(a) * b).astype(BF16)
    o_ref[...] = x + 0.5 * _dot(act_ref[...], wout_ref[...])


def _ffn(x, g, w_in, w_out, layer):
    n, d = x.shape
    d_ff = w_out.shape[1]
    return pl.pallas_call(
        _ffn_kernel,
        grid=(n // ROW_TILE,),
        in_specs=[
            pl.BlockSpec((ROW_TILE, d), lambda i: (i, 0)),
            pl.BlockSpec((None, 1, d), lambda i: (layer, 0, 0)),
            _resident((None, d, 2 * d_ff), lambda i: (layer, 0, 0)),
            _resident((None, d_ff, d), lambda i: (layer, 0, 0)),
        ],
        out_specs=pl.BlockSpec((ROW_TILE, d), lambda i: (i, 0)),
        out_shape=jax.ShapeDtypeStruct((n, d), F32),
        scratch_shapes=[pltpu.VMEM((ROW_TILE, d_ff), BF16)],
        compiler_params=_params("parallel"),
        name="ffn",
    )(x, g, w_in, w_out)


def _ple_kernel(x_ref, p_ref, g_ref, wg_ref, wp_ref, o_ref):
    x = x_ref[...]
    h = _rms(x, g_ref[...]).astype(BF16)
    gate = jax.nn.sigmoid(_dot(h, wg_ref[...]))
    o_ref[...] = x + gate * _dot(p_ref[...].astype(BF16), wp_ref[...])


def _ple(x, p, g, w_gate, w_proj, layer):
    n, d = x.shape
    dp = p.shape[-1]
    return pl.pallas_call(
        _ple_kernel,
        grid=(n // ROW_TILE,),
        in_specs=[
            pl.BlockSpec((ROW_TILE, d), lambda i: (i, 0)),
            pl.BlockSpec((None, ROW_TILE, dp), lambda i: (layer, i, 0)),
            pl.BlockSpec((None, 1, d), lambda i: (layer, 0, 0)),
            _resident((None, d, d), lambda i: (layer, 0, 0)),
            _resident((None, dp, d), lambda i: (layer, 0, 0)),
        ],
        out_specs=pl.BlockSpec((ROW_TILE, d), lambda i: (i, 0)),
        out_shape=jax.ShapeDtypeStruct((n, d), F32),
        compiler_params=_params("parallel"),
        name="ple",
    )(x, p, g, w_gate, w_proj)


def _final_norm_kernel(x_ref, g_ref, o_ref):
    o_ref[...] = _rms(x_ref[...], g_ref[...])


def _final_norm(x, g):
    n, d = x.shape
    return pl.pallas_call(
        _final_norm_kernel,
        grid=(n // ROW_TILE,),
        in_specs=[pl.BlockSpec((ROW_TILE, d), lambda i: (i, 0)), pl.BlockSpec((1, d), lambda i: (0, 0))],
        out_specs=pl.BlockSpec((ROW_TILE, d), lambda i: (i, 0)),
        out_shape=jax.ShapeDtypeStruct((n, d), F32),
        compiler_params=_params("parallel"),
        name="final_norm",
    )(x, g)


def _out_proj_kernel(*refs):
    x_ref, part_refs, w_ref, o_ref = refs[0], refs[1:-2], refs[-2], refs[-1]
    acc = x_ref[...]
    lo = 0
    for p_ref in part_refs:
        width = p_ref.shape[1]
        acc = acc + _dot(p_ref[...].astype(BF16), w_ref[lo:lo + width, :])
        lo += width
    o_ref[...] = acc


def _out_proj(x, parts, w, layer):
    n, d = x.shape
    return pl.pallas_call(
        _out_proj_kernel,
        grid=(n // ROW_TILE,),
        in_specs=[pl.BlockSpec((ROW_TILE, d), lambda i: (i, 0))]
        + [pl.BlockSpec((ROW_TILE, p.shape[1]), lambda i: (i, 0)) for p in parts]
        + [_resident((None, w.shape[1], d), lambda i: (layer, 0, 0))],
        out_specs=pl.BlockSpec((ROW_TILE, d), lambda i: (i, 0)),
        out_shape=jax.ShapeDtypeStruct((n, d), F32),
        compiler_params=_params("parallel"),
        name="out_proj",
    )(x, *parts, w)


def _rope_rows(y, cos, sin_signed, first_half):
    width = y.shape[1]
    partner = jnp.where(first_half, pltpu.roll(y, width - HALF, axis=1), pltpu.roll(y, HALF, axis=1))
    return y * cos + partner * sin_signed


def _rope_store_t(z, cos_t, sin_t, out_ref, n_heads):
    for h in range(n_heads):
        x1 = z[h * HEAD_DIM:h * HEAD_DIM + HALF]
        x2 = z[h * HEAD_DIM + HALF:(h + 1) * HEAD_DIM]
        out_ref[h * HEAD_DIM:h * HEAD_DIM + HALF, :] = x1 * cos_t - x2 * sin_t
        out_ref[h * HEAD_DIM + HALF:(h + 1) * HEAD_DIM, :] = x2 * cos_t + x1 * sin_t


def _proj_ab_kernel(x_ref, g_ref, wr_ref, wt_ref, cos_ref, sin_ref, cos_t_ref, sin_t_ref,
                    qa_ref, ka_ref, va_ref, qb_ref, ka_t_ref, va_t_ref, kb_t_ref, vb_t_ref):
    h = _rms(x_ref[...], g_ref[...]).astype(BF16)
    rows = h.shape[0]
    reps = A_W // LANES
    cos = jnp.concatenate([cos_ref[...]] * reps, axis=1)
    sin = jnp.concatenate([sin_ref[...]] * reps, axis=1)
    lane = lax.broadcasted_iota(jnp.int32, (rows, A_W), 1)
    first_half = (lane & (HEAD_DIM - 1)) < HALF
    rope = functools.partial(_rope_rows, cos=cos, sin_signed=sin, first_half=first_half)
    qa_ref[...] = rope(_dot(h, wr_ref[:, 0:A_W])) * ATTN_SCALE
    ka_ref[...] = rope(_dot(h, wr_ref[:, A_W:2 * A_W]))
    va_ref[...] = _dot(h, wr_ref[:, 2 * A_W:3 * A_W])
    qb_ref[...] = rope(_dot(h, wr_ref[:, 3 * A_W:3 * A_W + BQ_W])) * ATTN_SCALE
    cos_t = cos_t_ref[...]
    sin_t = sin_t_ref[...]
    _rope_store_t(_dot_nt(wt_ref[0:A_W, :], h), cos_t, sin_t, ka_t_ref, N_HEADS_A)
    va_t_ref[...] = _dot_nt(wt_ref[A_W:2 * A_W, :], h)
    _rope_store_t(_dot_nt(wt_ref[2 * A_W:2 * A_W + BKV_W, :], h), cos_t, sin_t, kb_t_ref, N_KV_B)
    vb_t_ref[...] = _dot_nt(wt_ref[2 * A_W + BKV_W:2 * A_W + 2 * BKV_W, :], h)


def _proj_ab(x, g, w_rows, w_t, tables, layer, e):
    n, d = x.shape
    cos, sin, cos_t, sin_t = tables
    row = lambda w: pl.BlockSpec((ROW_TILE, w), lambda i: (i, 0))
    col = lambda w: pl.BlockSpec((w, ROW_TILE), lambda i: (0, i))
    sds = jax.ShapeDtypeStruct
    return pl.pallas_call(
        _proj_ab_kernel,
        grid=(n // ROW_TILE,),
        in_specs=[
            row(d),
            pl.BlockSpec((None, 1, d), lambda i: (layer, 0, 0)),
            _resident((None, d, w_rows.shape[2]), lambda i: (e, 0, 0)),
            _resident((None, w_t.shape[1], d), lambda i: (e, 0, 0)),
            row(LANES), row(LANES), col(HALF), col(HALF),
        ],
        out_specs=[row(A_W), row(A_W), row(A_W), row(BQ_W), col(A_W), col(A_W), col(BKV_W), col(BKV_W)],
        out_shape=[sds((n, A_W), F32)] * 3 + [sds((n, BQ_W), F32)]
        + [sds((A_W, n), F32)] * 2 + [sds((BKV_W, n), F32)] * 2,
        compiler_params=_params("parallel"),
        name="proj_ab",
    )(x, g, w_rows, w_t, cos, sin, cos_t, sin_t)


def _proj_c_kernel(x_ref, g_ref, wq_ref, wt_ref, bf_ref, q_ref, k_t_ref, v_t_ref, lf_t_ref):
    h = _rms(x_ref[...], g_ref[...]).astype(BF16)
    q_ref[...] = _dot(h, wq_ref[...]) * ATTN_SCALE
    k_t_ref[...] = _dot_nt(wt_ref[0:CKV_W, :], h)
    v_t_ref[...] = _dot_nt(wt_ref[CKV_W:2 * CKV_W, :], h)
    f_t = _dot_nt(wt_ref[2 * CKV_W:2 * CKV_W + N_HEADS_C, :], h)
    lf_t_ref[...] = jax.nn.log_sigmoid(f_t + bf_ref[...])


def _proj_c(x, g, w_q, w_t, b_forget, layer, o):
    n, d = x.shape
    row = lambda w: pl.BlockSpec((ROW_TILE, w), lambda i: (i, 0))
    col = lambda w: pl.BlockSpec((w, ROW_TILE), lambda i: (0, i))
    sds = jax.ShapeDtypeStruct
    return pl.pallas_call(
        _proj_c_kernel,
        grid=(n // ROW_TILE,),
        in_specs=[
            row(d),
            pl.BlockSpec((None, 1, d), lambda i: (layer, 0, 0)),
            _resident((None, d, CQ_W), lambda i: (o, 0, 0)),
            _resident((None, w_t.shape[1], d), lambda i: (o, 0, 0)),
            pl.BlockSpec((None, N_HEADS_C, 1), lambda i: (o, 0, 0)),
        ],
        out_specs=[row(CQ_W), col(CKV_W), col(CKV_W), col(N_HEADS_C)],
        out_shape=[sds((n, CQ_W), F32), sds((CKV_W, n), F32), sds((CKV_W, n), F32), sds((N_HEADS_C, n), F32)],
        compiler_params=_params("parallel"),
        name="proj_c",
    )(x, g, w_q, w_t, b_forget)


def _softmax_stats_2blk(sp, sc):
    m = jnp.maximum(jnp.max(sp, axis=1, keepdims=True), jnp.max(sc, axis=1, keepdims=True))
    ep = jnp.exp(sp - m)
    ec = jnp.exp(sc - m)
    den = jnp.sum(ep, axis=1, keepdims=True) + jnp.sum(ec, axis=1, keepdims=True)
    return m, ep, ec, den


def _prompt_a_kernel(q_ref, k_ref, v_ref, o_ref, num_ref, den_ref, m_ref):
    seq = q_ref.shape[0]
    ri = lax.broadcasted_iota(jnp.int32, (BAND, BAND), 0)
    ci = lax.broadcasted_iota(jnp.int32, (BAND, BAND), 1)
    prev_ok = ci >= ri
    cur_ok = ci <= ri
    low = lax.broadcasted_iota(jnp.int32, (BAND, LANES), 1) < HEAD_DIM

    for pi, (window, r) in enumerate(A_PATTERNS):
        assert window // r == BAND
        nblk = seq // (r * BAND)

        def rows(start, r=r):
            return pl.ds(start, BAND) if r == 1 else pl.ds(start, BAND, stride=r)

        def body(idx, carry, r=r, nblk=nblk, pi=pi, rows=rows):
            rho = idx // nblk
            n = idx - rho * nblk
            q_rows = rows(rho + r * BAND * n)
            p_rows = rows(rho + r * BAND * jnp.maximum(n - 1, 0))
            q = q_ref[q_rows, :]
            kc, vc = k_ref[q_rows, :], v_ref[q_rows, :]
            kp, vp = k_ref[p_rows, :], v_ref[p_rows, :]
            no_prev = jnp.where(n > 0, 0.0, NEG)
            nums, dens, ms = [], [], []
            for h in range(LANES // HEAD_DIM):
                sl = slice(h * HEAD_DIM, (h + 1) * HEAD_DIM)
                qh = q[:, sl].astype(BF16)
                sp = jnp.where(prev_ok, _dot_nt(qh, kp[:, sl].astype(BF16)), NEG) + no_prev
                sc = jnp.where(cur_ok, _dot_nt(qh, kc[:, sl].astype(BF16)), NEG)
                m, ep, ec, den = _softmax_stats_2blk(sp, sc)
                nums.append(_dot(ep.astype(BF16), vp[:, sl].astype(BF16))
                            + _dot(ec.astype(BF16), vc[:, sl].astype(BF16)))
                dens.append(den)
                ms.append(m)
            num = jnp.concatenate(nums, axis=1)
            den = jnp.where(low, dens[0], dens[1])
            m = jnp.where(low, ms[0], ms[1])
            if pi > 0:
                m_old = m_ref[q_rows, :]
                big = jnp.maximum(m_old, m)
                a_old = jnp.exp(m_old - big)
                a_new = jnp.exp(m - big)
                num = a_old * num_ref[q_rows, :] + a_new * num
                den = a_old * den_ref[q_rows, :] + a_new * den
                m = big
            num_ref[q_rows, :] = num
            den_ref[q_rows, :] = den
            m_ref[q_rows, :] = m
            return carry

        lax.fori_loop(0, r * nblk, body, 0)

    def finish(i, carry):
        sl = pl.ds(pl.multiple_of(i * BAND, BAND), BAND)
        o_ref[sl, :] = num_ref[sl, :] / den_ref[sl, :]
        return carry

    lax.fori_loop(0, seq // BAND, finish, 0)


def _prompt_a(qa, ka, va, n_seq, seq):
    n = qa.shape[0]
    blk = pl.BlockSpec((seq, LANES), lambda b, hp: (b, hp))
    return pl.pallas_call(
        _prompt_a_kernel,
        grid=(n_seq, A_W // LANES),
        in_specs=[blk, blk, blk],
        out_specs=blk,
        out_shape=jax.ShapeDtypeStruct((n, A_W), F32),
        scratch_shapes=[pltpu.VMEM((seq, LANES), F32)] * 3,
        compiler_params=_params("parallel", "parallel"),
        name="prompt_a",
    )(qa, ka, va)


def _prompt_b_kernel(sink_ref, q_ref, k_t_ref, v_t_ref, o_ref):
    seq = q_ref.shape[0]
    g = pl.program_id(1)
    ri = lax.broadcasted_iota(jnp.int32, (BAND, BAND), 0)
    ci = lax.broadcasted_iota(jnp.int32, (BAND, BAND), 1)
    prev_ok = ci >= ri
    cur_ok = ci <= ri

    def body(n, carry):
        cur = pl.ds(pl.multiple_of(n * BAND, BAND), BAND)
        prev = pl.ds(pl.multiple_of(jnp.maximum(n - 1, 0) * BAND, BAND), BAND)
        q = q_ref[cur, :]
        kc = k_t_ref[:, cur].astype(BF16)
        vc = v_t_ref[:, cur].astype(BF16)
        kp = k_t_ref[:, prev].astype(BF16)
        vp = v_t_ref[:, prev].astype(BF16)
        no_prev = jnp.where(n > 0, 0.0, NEG)
        outs = []
        for hh in range(G_B):
            qh = q[:, hh * HEAD_DIM:(hh + 1) * HEAD_DIM].astype(BF16)
            sp = jnp.where(prev_ok, _dot(qh, kp), NEG) + no_prev
            sc = jnp.where(cur_ok, _dot(qh, kc), NEG)
            m, ep, ec, den = _softmax_stats_2blk(sp, sc)
            num = _dot_nt(ep.astype(BF16), vp) + _dot_nt(ec.astype(BF16), vc)
            sink = sink_ref[g * G_B + hh]
            big = jnp.maximum(m, sink)
            a = jnp.exp(m - big)
            outs.append(num * a / (den * a + jnp.exp(sink - big)))
        o_ref[cur, :] = jnp.concatenate(outs, axis=1)
        return carry

    lax.fori_loop(0, seq // BAND, body, 0)


def _prompt_b(sink, qb, kb_t, vb_t, n_seq, seq):
    n = qb.shape[0]
    gw = G_B * HEAD_DIM
    q_spec = pl.BlockSpec((seq, gw), lambda b, g: (b, g))
    kv_spec = pl.BlockSpec((HEAD_DIM, seq), lambda b, g: (g, b))
    return pl.pallas_call(
        _prompt_b_kernel,
        grid=(n_seq, N_KV_B),
        in_specs=[pl.BlockSpec(memory_space=pltpu.SMEM), q_spec, kv_spec, kv_spec],
        out_specs=q_spec,
        out_shape=jax.ShapeDtypeStruct((n, BQ_W), F32),
        compiler_params=_params("parallel", "parallel"),
        name="prompt_b",
    )(sink, qb, kb_t, vb_t)


def _shifted_buffer(cache, new):
    return jnp.concatenate([cache[:, new.shape[1]:], new], axis=1)


def _sample_a_kernel(q_ref, ck_ref, cv_ref, nk_ref, nv_ref, *rest):
    o_ref, ok_ref, ov_ref = rest[-3:]
    t_new, buf = q_ref.shape[0], ck_ref.shape[1]
    q = q_ref[...].astype(BF16)
    kc, vc = ck_ref[...], cv_ref[...]
    kn, vn = nk_ref[...], nv_ref[...]

    def count(d):
        c = jnp.zeros(d.shape, F32)
        for window, r in A_PATTERNS:
            assert r & (r - 1) == 0
            c = c + jnp.where(((d & (r - 1)) == 0) & (d <= window) & (d >= 0), 1.0, 0.0)
        return c

    d_c = buf + lax.broadcasted_iota(jnp.int32, (t_new, buf), 0) - lax.broadcasted_iota(jnp.int32, (t_new, buf), 1)
    d_n = lax.broadcasted_iota(jnp.int32, (t_new, t_new), 0) - lax.broadcasted_iota(jnp.int32, (t_new, t_new), 1)
    w_c, w_n = count(d_c), count(d_n)
    s_c = jnp.where(w_c > 0, _dot(q, kc.astype(BF16)), NEG)
    s_n = jnp.where(w_n > 0, _dot(q, kn.astype(BF16)), NEG)
    m = jnp.maximum(jnp.max(s_c, axis=1, keepdims=True), jnp.max(s_n, axis=1, keepdims=True))
    e_c = jnp.exp(s_c - m) * w_c
    e_n = jnp.exp(s_n - m) * w_n
    den = jnp.sum(e_c, axis=1, keepdims=True) + jnp.sum(e_n, axis=1, keepdims=True)
    num = _dot_nt(e_c.astype(BF16), vc.astype(BF16)) + _dot_nt(e_n.astype(BF16), vn.astype(BF16))
    o_ref[...] = num / den
    ok_ref[...] = _shifted_buffer(kc, kn)
    ov_ref[...] = _shifted_buffer(vc, vn)


def _sample_a(q, cache_k_t, cache_v_t, new_k_t, new_v_t, e, prev_out):
    n_e, n_b, n_h, _, buf = cache_k_t.shape
    t_new = q.shape[2]
    q_spec = pl.BlockSpec((None, None, t_new, HEAD_DIM), lambda b, h: (b, h, 0, 0))
    c_spec = pl.BlockSpec((None, None, None, HEAD_DIM, buf), lambda b, h: (e, b, h, 0, 0))
    n_spec = pl.BlockSpec((None, None, HEAD_DIM, t_new), lambda b, h: (b, h, 0, 0))
    in_specs = [q_spec, c_spec, c_spec, n_spec, n_spec]
    args = [q, cache_k_t, cache_v_t, new_k_t, new_v_t]
    aliases = {}
    if prev_out is not None:
        in_specs += [pl.BlockSpec(memory_space=pl.ANY)] * 2
        args += list(prev_out)
        aliases = {5: 1, 6: 2}
    return pl.pallas_call(
        _sample_a_kernel,
        grid=(n_b, n_h),
        in_specs=in_specs,
        out_specs=[q_spec, c_spec, c_spec],
        out_shape=[jax.ShapeDtypeStruct(q.shape, F32),
                   jax.ShapeDtypeStruct(cache_k_t.shape, F32),
                   jax.ShapeDtypeStruct(cache_v_t.shape, F32)],
        input_output_aliases=aliases,
        compiler_params=_params("parallel", "parallel"),
        name="sample_a",
    )(*args)


def _sample_b_kernel(sink_ref, q_ref, ck_ref, cv_ref, nk_ref, nv_ref, *rest):
    o_ref, ok_ref, ov_ref = rest[-3:]
    rows, buf, t_new = q_ref.shape[1], ck_ref.shape[2], nk_ref.shape[2]
    t_c = lax.broadcasted_iota(jnp.int32, (rows, buf), 0) & (t_new - 1)
    d_c = buf + t_c - lax.broadcasted_iota(jnp.int32, (rows, buf), 1)
    t_n = lax.broadcasted_iota(jnp.int32, (rows, t_new), 0) & (t_new - 1)
    d_n = t_n - lax.broadcasted_iota(jnp.int32, (rows, t_new), 1)
    ok_c = d_c <= WINDOW_B
    ok_n = (d_n >= 0) & (d_n <= WINDOW_B)
    head = lax.broadcasted_iota(jnp.int32, (rows, 1), 0) // t_new
    for g in range(N_KV_B):
        q = q_ref[g].astype(BF16)
        kc, vc = ck_ref[g], cv_ref[g]
        kn, vn = nk_ref[g], nv_ref[g]
        s_c = jnp.where(ok_c, _dot(q, kc.astype(BF16)), NEG)
        s_n = jnp.where(ok_n, _dot(q, kn.astype(BF16)), NEG)
        m = jnp.maximum(jnp.max(s_c, axis=1, keepdims=True), jnp.max(s_n, axis=1, keepdims=True))
        e_c = jnp.exp(s_c - m)
        e_n = jnp.exp(s_n - m)
        den = jnp.sum(e_c, axis=1, keepdims=True) + jnp.sum(e_n, axis=1, keepdims=True)
        num = _dot_nt(e_c.astype(BF16), vc.astype(BF16)) + _dot_nt(e_n.astype(BF16), vn.astype(BF16))
        sink = jnp.zeros((rows, 1), F32)
        for hh in range(G_B):
            sink = jnp.where(head == hh, sink_ref[g * G_B + hh], sink)
        big = jnp.maximum(m, sink)
        a = jnp.exp(m - big)
        o_ref[g] = num * a / (den * a + jnp.exp(sink - big))
        ok_ref[g] = _shifted_buffer(kc, kn)
        ov_ref[g] = _shifted_buffer(vc, vn)


def _sample_b(sink, q, cache_k_t, cache_v_t, new_k_t, new_v_t, e, prev_out):
    n_e, n_b, n_kv, _, buf = cache_k_t.shape
    rows, t_new = q.shape[2], new_k_t.shape[3]
    assert t_new & (t_new - 1) == 0
    q_spec = pl.BlockSpec((None, n_kv, rows, HEAD_DIM), lambda b: (b, 0, 0, 0))
    c_spec = pl.BlockSpec((None, None, n_kv, HEAD_DIM, buf), lambda b: (e, b, 0, 0, 0))
    n_spec = pl.BlockSpec((None, n_kv, HEAD_DIM, t_new), lambda b: (b, 0, 0, 0))
    in_specs = [pl.BlockSpec(memory_space=pltpu.SMEM), q_spec, c_spec, c_spec, n_spec, n_spec]
    args = [sink, q, cache_k_t, cache_v_t, new_k_t, new_v_t]
    aliases = {}
    if prev_out is not None:
        in_specs += [pl.BlockSpec(memory_space=pl.ANY)] * 2
        args += list(prev_out)
        aliases = {6: 1, 7: 2}
    return pl.pallas_call(
        _sample_b_kernel,
        grid=(n_b,),
        in_specs=in_specs,
        out_specs=[q_spec, c_spec, c_spec],
        out_shape=[jax.ShapeDtypeStruct(q.shape, F32),
                   jax.ShapeDtypeStruct(cache_k_t.shape, F32),
                   jax.ShapeDtypeStruct(cache_v_t.shape, F32)],
        input_output_aliases=aliases,
        compiler_params=_params("parallel"),
        name="sample_b",
    )(*args)


def _upper_ones(n):
    return jnp.where(lax.broadcasted_iota(jnp.int32, (n, n), 0) <= lax.broadcasted_iota(jnp.int32, (n, n), 1),
                     1.0, 0.0).astype(F32)


def _cum_kernel(lf_ref, o_ref, carry_ref):
    @pl.when(pl.program_id(1) == 0)
    def _():
        carry_ref[...] = jnp.zeros_like(carry_ref)

    width = lf_ref.shape[1]
    c = _dot_exact(lf_ref[...], _upper_ones(width)) + carry_ref[:, 0:1]
    o_ref[...] = c
    carry_ref[...] = jnp.broadcast_to(c[:, width - 1:width], carry_ref.shape)


def _prompt_cum(lf_t, n_seq, seq):
    heads = lf_t.shape[0]
    per = seq // CUM_BLOCK
    spec = pl.BlockSpec((heads, CUM_BLOCK), lambda b, j: (0, b * per + j))
    return pl.pallas_call(
        _cum_kernel,
        grid=(n_seq, per),
        in_specs=[spec],
        out_specs=spec,
        out_shape=jax.ShapeDtypeStruct((heads, n_seq * seq), F32),
        scratch_shapes=[pltpu.VMEM((heads, LANES), F32)],
        compiler_params=_params("parallel", "arbitrary"),
        name="prompt_cum",
    )(lf_t)


def _fox_prompt_kernel(q_ref, k_t_ref, v_t_ref, cum_ref, o_ref, m_ref, l_ref, acc_ref):
    i, j = pl.program_id(1), pl.program_id(2)
    tq, tk = q_ref.shape[0], k_t_ref.shape[1]

    @pl.when(j == 0)
    def _():
        m_ref[...] = jnp.full(m_ref.shape, NEG, F32)
        l_ref[...] = jnp.zeros_like(l_ref)
        acc_ref[...] = jnp.zeros_like(acc_ref)

    @pl.when(j * tk <= i * tq + tq - 1)
    def _():
        q_pos = i * tq + lax.broadcasted_iota(jnp.int32, (tq, tk), 0)
        k_pos = j * tk + lax.broadcasted_iota(jnp.int32, (tq, tk), 1)
        causal = k_pos <= q_pos
        for g in range(N_KV_C):
            k_t = k_t_ref[g * HEAD_DIM:(g + 1) * HEAD_DIM, :].astype(BF16)
            v_t = v_t_ref[g * HEAD_DIM:(g + 1) * HEAD_DIM, :].astype(BF16)
            for hh in range(G_C):
                h = g * G_C + hh
                qh = q_ref[:, h * HEAD_DIM:(h + 1) * HEAD_DIM].astype(BF16)
                s = jnp.where(causal, _dot(qh, k_t) - cum_ref[h:h + 1, :], NEG)
                m_old = m_ref[h]
                m_new = jnp.maximum(m_old, jnp.max(s, axis=1, keepdims=True))
                alpha = jnp.exp(m_old - m_new)
                p = jnp.exp(s - m_new)
                l_ref[h] = alpha * l_ref[h] + jnp.sum(p, axis=1, keepdims=True)
                acc_ref[h] = alpha * acc_ref[h] + _dot_nt(p.astype(BF16), v_t)
                m_ref[h] = m_new

    @pl.when(j == pl.num_programs(2) - 1)
    def _():
        o_ref[...] = jnp.concatenate([acc_ref[h] / l_ref[h] for h in range(N_HEADS_C)], axis=1)


def _fox_prompt(q, k_t, v_t, cum_t, n_seq, seq):
    n = q.shape[0]
    nq, nk = seq // FOX_TQ, seq // FOX_TK

    def kv_idx(b, i, j):
        return (0, b * nk + jnp.minimum(j, (i * FOX_TQ + FOX_TQ - 1) // FOX_TK))

    q_spec = pl.BlockSpec((FOX_TQ, CQ_W), lambda b, i, j: (b * nq + i, 0))
    return pl.pallas_call(
        _fox_prompt_kernel,
        grid=(n_seq, nq, nk),
        in_specs=[q_spec,
                  pl.BlockSpec((CKV_W, FOX_TK), kv_idx),
                  pl.BlockSpec((CKV_W, FOX_TK), kv_idx),
                  pl.BlockSpec((N_HEADS_C, FOX_TK), kv_idx)],
        out_specs=q_spec,
        out_shape=jax.ShapeDtypeStruct((n, CQ_W), F32),
        scratch_shapes=[pltpu.VMEM((N_HEADS_C, FOX_TQ, 1), F32),
                        pltpu.VMEM((N_HEADS_C, FOX_TQ, 1), F32),
                        pltpu.VMEM((N_HEADS_C, FOX_TQ, HEAD_DIM), F32)],
        compiler_params=_params("parallel", "parallel", "arbitrary"),
        name="fox_prompt",
    )(q, k_t, v_t, cum_t)


def _fox_sample_kernel(pt_ref, q_ref, *refs):
    npg = PAGES_PER_STEP
    k_pages, v_pages, lf_pages = refs[0:npg], refs[npg:2 * npg], refs[2 * npg:3 * npg]
    nk_ref, nv_ref, nlf_ref, o_ref, m_ref, l_ref, acc_ref, pref_ref = refs[3 * npg:]
    j = pl.program_id(1)
    page = lf_pages[0].shape[1]
    t_new = nk_ref.shape[2]
    rows_g = G_C * t_new

    @pl.when(j == 0)
    def _():
        m_ref[...] = jnp.full(m_ref.shape, NEG, F32)
        l_ref[...] = jnp.zeros_like(l_ref)
        acc_ref[...] = jnp.zeros_like(acc_ref)
        pref_ref[...] = jnp.zeros_like(pref_ref)

    def per_row(c):
        return jnp.broadcast_to(c[:, None, :], (G_C, t_new, c.shape[1])).reshape(rows_g, c.shape[1])

    def update(g, s_parts, v_parts):
        sl = slice(g * rows_g, (g + 1) * rows_g)
        m_old = m_ref[sl, :]
        m_new = m_old
        for s in s_parts:
            m_new = jnp.maximum(m_new, jnp.max(s, axis=1, keepdims=True))
        alpha = jnp.exp(m_old - m_new)
        l_new = alpha * l_ref[sl, :]
        acc = alpha * acc_ref[sl, :]
        for s, v_t in zip(s_parts, v_parts):
            p = jnp.exp(s - m_new)
            l_new = l_new + jnp.sum(p, axis=1, keepdims=True)
            acc = acc + _dot_nt(p.astype(BF16), v_t.astype(BF16))
        m_ref[sl, :] = m_new
        l_ref[sl, :] = l_new
        acc_ref[sl, :] = acc

    tri = _upper_ones(page)
    pref = pref_ref[:, 0:1]
    cums = []
    for i in range(npg):
        c = _dot_exact(lf_pages[i][...], tri) + pref
        cums.append(c)
        pref = c[:, page - 1:page]
    pref_ref[...] = jnp.broadcast_to(pref, pref_ref.shape)

    for g in range(N_KV_C):
        q = q_ref[g * rows_g:(g + 1) * rows_g, :].astype(BF16)
        s_parts = [_dot(q, k_pages[i][g].astype(BF16)) - per_row(cums[i][g * G_C:(g + 1) * G_C]) for i in range(npg)]
        update(g, s_parts, [v_pages[i][g] for i in range(npg)])

    @pl.when(j == pl.num_programs(1) - 1)
    def _():
        lf = nlf_ref[...]
        lane = lax.broadcasted_iota(jnp.int32, lf.shape, 1)
        cum_n = jnp.broadcast_to(pref, lf.shape)
        for t in range(t_new):
            cum_n = cum_n + jnp.where(lane >= t, lf[:, t:t + 1], 0.0)
        tok = lax.broadcasted_iota(jnp.int32, (rows_g, t_new), 0) & (t_new - 1)
        causal = lax.broadcasted_iota(jnp.int32, (rows_g, t_new), 1) <= tok
        for g in range(N_KV_C):
            q = q_ref[g * rows_g:(g + 1) * rows_g, :].astype(BF16)
            s = _dot(q, nk_ref[g].astype(BF16)) - per_row(cum_n[g * G_C:(g + 1) * G_C])
            update(g, [jnp.where(causal, s, NEG)], [nv_ref[g]])
        o_ref[...] = acc_ref[...] / l_ref[...]


def _fox_sample(page_table, q, cache_k_t, cache_v_t, cache_lf_t, new_k_t, new_v_t, new_lf_t, o):
    n_b, rows, _ = q.shape
    page = cache_k_t.shape[-1]
    t_new = new_k_t.shape[-1]
    n_pages = page_table.shape[1]
    npg = PAGES_PER_STEP
    assert t_new & (t_new - 1) == 0 and n_pages % npg == 0

    def page_spec(shape, i):
        return pl.BlockSpec((None, None) + shape, lambda b, j, pt: (o, pt[b, j * npg + i]) + (0,) * len(shape))

    q_spec = pl.BlockSpec((None, rows, HEAD_DIM), lambda b, j, pt: (b, 0, 0))
    in_specs = [q_spec]
    in_specs += [page_spec((N_KV_C, HEAD_DIM, page), i) for i in range(npg)]
    in_specs += [page_spec((N_KV_C, HEAD_DIM, page), i) for i in range(npg)]
    in_specs += [page_spec((N_HEADS_C, page), i) for i in range(npg)]
    in_specs += [pl.BlockSpec((None, N_KV_C, HEAD_DIM, t_new), lambda b, j, pt: (b, 0, 0, 0))] * 2
    in_specs += [pl.BlockSpec((None, N_HEADS_C, t_new), lambda b, j, pt: (b, 0, 0))]
    grid_spec = pltpu.PrefetchScalarGridSpec(
        num_scalar_prefetch=1,
        grid=(n_b, n_pages // npg),
        in_specs=in_specs,
        out_specs=q_spec,
        scratch_shapes=[pltpu.VMEM((rows, 1), F32), pltpu.VMEM((rows, 1), F32),
                        pltpu.VMEM((rows, HEAD_DIM), F32), pltpu.VMEM((N_HEADS_C, LANES), F32)],
    )
    return pl.pallas_call(
        _fox_sample_kernel,
        grid_spec=grid_spec,
        out_shape=jax.ShapeDtypeStruct(q.shape, F32),
        compiler_params=_params("parallel", "arbitrary"),
        name="fox_sample",
    )(page_table, q, *([cache_k_t] * npg), *([cache_v_t] * npg), *([cache_lf_t] * npg),
      new_k_t, new_v_t, new_lf_t)


def _rope_tables(seq, n_sample_seq, t_new, past_len):
    inv_freq = ROPE_THETA ** (-jnp.arange(HALF, dtype=F32) / HALF)

    def tables(pos):
        ang = pos.astype(F32)[:, None] * inv_freq[None, :]
        return jnp.cos(ang), jnp.sin(ang)

    cos_p, sin_p = tables(jnp.arange(seq))
    cos_s, sin_s = tables(past_len + jnp.arange(t_new))
    return cos_p, sin_p, cos_s, sin_s


def kernel(x_prompt, x_sample, cache_a_k, cache_a_v, cache_b_k, cache_b_v, cache_c_k, cache_c_v, cache_c_logf, page_table, p_prompt, p_sample, g_ffn1, w_ffn1_in, w_ffn1_out, g_mix, w_in_ab, w_out_ab, sink_b, w_in_c, b_forget, w_out_c, g_ffn2, w_ffn2_in, w_ffn2_out, g_ple, w_ple_gate, w_ple_proj, g_final):
    n_seq, seq, d = x_prompt.shape
    n_dec, t_new, _ = x_sample.shape
    depth = g_ffn1.shape[0]
    n_p, n_s = n_seq * seq, n_dec * t_new
    n = n_p + n_s
    past_len = page_table.shape[1] * cache_c_k.shape[2]
    assert n_p % ROW_TILE == 0 and n_s % ROW_TILE == 0 and seq % ROW_TILE == 0

    x = jnp.concatenate([x_prompt.reshape(n_p, d), x_sample.reshape(n_s, d)], axis=0)
    p_all = jnp.concatenate([p_prompt.reshape(depth, n_p, -1), p_sample.reshape(depth, n_s, -1)], axis=1)

    cos_p, sin_p, cos_s, sin_s = _rope_tables(seq, n_dec, t_new, past_len)
    cos = jnp.concatenate([jnp.tile(cos_p, (n_seq, 1)), jnp.tile(cos_s, (n_dec, 1))], axis=0)
    sin = jnp.concatenate([jnp.tile(sin_p, (n_seq, 1)), jnp.tile(sin_s, (n_dec, 1))], axis=0)
    reps = LANES // HEAD_DIM
    tables = (jnp.tile(cos, (1, 2 * reps)), jnp.tile(jnp.concatenate([-sin, sin], axis=1), (1, reps)), cos.T, sin.T)

    bf = lambda w: w.astype(BF16)
    g3 = lambda g: g.reshape(depth, 1, d)
    g_ffn1, g_mix, g_ffn2, g_ple = g3(g_ffn1), g3(g_mix), g3(g_ffn2), g3(g_ple)
    w_ffn1_in, w_ffn1_out, w_ffn2_in, w_ffn2_out = bf(w_ffn1_in), bf(w_ffn1_out), bf(w_ffn2_in), bf(w_ffn2_out)
    w_ple_gate, w_ple_proj, w_out_ab, w_out_c = bf(w_ple_gate), bf(w_ple_proj), bf(w_out_ab), bf(w_out_c)
    qb_lo, kb_lo = 3 * A_W, 3 * A_W + BQ_W
    w_ab_rows = bf(w_in_ab[:, :, :kb_lo])
    w_ab_t = bf(jnp.swapaxes(jnp.concatenate([w_in_ab[:, :, A_W:3 * A_W], w_in_ab[:, :, kb_lo:]], axis=2), 1, 2))
    w_c_q = bf(w_in_c[:, :, :CQ_W])
    w_c_t = bf(jnp.swapaxes(w_in_c[:, :, CQ_W:], 1, 2))
    b_forget3 = b_forget.reshape(b_forget.shape[0], N_HEADS_C, 1)

    tr = lambda c: jnp.transpose(c, (0, 1, 3, 4, 2))
    cache_a_k_t, cache_a_v_t, cache_b_k_t, cache_b_v_t = tr(cache_a_k), tr(cache_a_v), tr(cache_b_k), tr(cache_b_v)
    cache_c_k_t, cache_c_v_t = tr(cache_c_k), tr(cache_c_v)
    cache_c_lf_t = jnp.transpose(cache_c_logf, (0, 1, 3, 2))

    def sample_cols(a_t, heads):
        return a_t[:, n_p:].reshape(heads, HEAD_DIM, n_dec, t_new).transpose(2, 0, 1, 3)

    def prompt_cols(a_t, heads, keep):
        a = a_t[:, :n_p].reshape(heads, HEAD_DIM, n_seq, seq)[..., seq - keep:]
        return a.transpose(2, 3, 0, 1)

    buf_a, buf_b = cache_a_k.shape[2], cache_b_k.shape[2]
    keep_a, keep_b = min(buf_a, seq), min(buf_b, seq)
    ak_p, av_p, bk_p, bv_p = [], [], [], []
    ck_p, cv_p, cl_p, ck_s, cv_s, cl_s = [], [], [], [], [], []
    a_out, b_out = None, None

    for l in range(depth):
        x = _ffn(x, g_ffn1, w_ffn1_in, w_ffn1_out, l)
        if l % 2 == 0:
            e = l // 2
            qa, ka, va, qb, ka_t, va_t, kb_t, vb_t = _proj_ab(x, g_mix, w_ab_rows, w_ab_t, tables, l, e)
            oa = _prompt_a(qa, ka, va, n_seq, seq)
            ob = _prompt_b(sink_b[e], qb, kb_t, vb_t, n_seq, seq)
            qa_s = qa[n_p:].reshape(n_dec, t_new, N_HEADS_A, HEAD_DIM).transpose(0, 2, 1, 3)
            a_res = _sample_a(qa_s, cache_a_k_t, cache_a_v_t, sample_cols(ka_t, N_HEADS_A),
                              sample_cols(va_t, N_HEADS_A), e, a_out)
            oa_s, a_out = a_res[0], a_res[1:]
            qb_s = qb[n_p:].reshape(n_dec, t_new, N_KV_B, G_B, HEAD_DIM).transpose(0, 2, 3, 1, 4)
            qb_s = qb_s.reshape(n_dec, N_KV_B, G_B * t_new, HEAD_DIM)
            b_res = _sample_b(sink_b[e], qb_s, cache_b_k_t, cache_b_v_t, sample_cols(kb_t, N_KV_B),
                              sample_cols(vb_t, N_KV_B), e, b_out)
            ob_s, b_out = b_res[0], b_res[1:]
            oa_s = oa_s.transpose(0, 2, 1, 3).reshape(n_s, A_W)
            ob_s = ob_s.reshape(n_dec, N_KV_B, G_B, t_new, HEAD_DIM).transpose(0, 3, 1, 2, 4).reshape(n_s, BQ_W)
            oa = lax.dynamic_update_slice(oa, oa_s, (n_p, 0))
            ob = lax.dynamic_update_slice(ob, ob_s, (n_p, 0))
            x = _out_proj(x, [oa, ob], w_out_ab, e)
            ak_p.append(prompt_cols(ka_t, N_HEADS_A, keep_a))
            av_p.append(prompt_cols(va_t, N_HEADS_A, keep_a))
            bk_p.append(prompt_cols(kb_t, N_KV_B, keep_b))
            bv_p.append(prompt_cols(vb_t, N_KV_B, keep_b))
        else:
            o = l // 2
            qc, kc_t, vc_t, lf_t = _proj_c(x, g_mix, w_c_q, w_c_t, b_forget3, l, o)
            cum_t = _prompt_cum(lf_t[:, :n_p], n_seq, seq)
            oc = _fox_prompt(qc, kc_t, vc_t, cum_t, n_seq, seq)
            qc_s = qc[n_p:].reshape(n_dec, t_new, N_HEADS_C, HEAD_DIM).transpose(0, 2, 1, 3)
            qc_s = qc_s.reshape(n_dec, N_HEADS_C * t_new, HEAD_DIM)
            new_lf_t = lf_t[:, n_p:].reshape(N_HEADS_C, n_dec, t_new).transpose(1, 0, 2)
            oc_s = _fox_sample(page_table, qc_s, cache_c_k_t, cache_c_v_t, cache_c_lf_t,
                               sample_cols(kc_t, N_KV_C), sample_cols(vc_t, N_KV_C), new_lf_t, o)
            oc_s = oc_s.reshape(n_dec, N_HEADS_C, t_new, HEAD_DIM).transpose(0, 2, 1, 3).reshape(n_s, CQ_W)
            oc = lax.dynamic_update_slice(oc, oc_s, (n_p, 0))
            x = _out_proj(x, [oc], w_out_c, o)
            ck_p.append(prompt_cols(kc_t, N_KV_C, seq))
            cv_p.append(prompt_cols(vc_t, N_KV_C, seq))
            cl_p.append(lf_t[:, :n_p].reshape(N_HEADS_C, n_seq, seq).transpose(1, 2, 0))
            ck_s.append(sample_cols(kc_t, N_KV_C).transpose(0, 3, 1, 2))
            cv_s.append(sample_cols(vc_t, N_KV_C).transpose(0, 3, 1, 2))
            cl_s.append(lf_t[:, n_p:].reshape(N_HEADS_C, n_dec, t_new).transpose(1, 2, 0))
        x = _ffn(x, g_ffn2, w_ffn2_in, w_ffn2_out, l)
        x = _ple(x, p_all, g_ple, w_ple_gate, w_ple_proj, l)

    y = _final_norm(x, g_final.reshape(1, d))
    back = lambda c: jnp.transpose(c, (0, 1, 4, 2, 3))
    return (y[:n_p].reshape(n_seq, seq, d), y[n_p:].reshape(n_dec, t_new, d),
            jnp.stack(ak_p), jnp.stack(av_p), back(a_out[0]), back(a_out[1]),
            jnp.stack(bk_p), jnp.stack(bv_p), back(b_out[0]), back(b_out[1]),
            jnp.stack(ck_p), jnp.stack(cv_p), jnp.stack(cl_p),
            jnp.stack(ck_s), jnp.stack(cv_s), jnp.stack(cl_s))
```

```python
import functools

import jax
import jax.numpy as jnp
from jax import lax
from jax.experimental import pallas as pl
from jax.experimental.pallas import tpu as pltpu

F32 = jnp.float32
BF16 = jnp.bfloat16

HEAD_DIM = 64
HALF = HEAD_DIM // 2
ATTN_SCALE = HEAD_DIM ** -0.5
A_PATTERNS = ((128, 1), (512, 4), (2048, 16))
N_HEADS_A = 8
N_HEADS_B = 8
N_KV_B = 2
G_B = N_HEADS_B // N_KV_B
WINDOW_B = 128
N_HEADS_C = 16
N_KV_C = 4
G_C = N_HEADS_C // N_KV_C
ROPE_THETA = 10000.0
RMS_EPS = 1e-6
NEG = -1e30

A_W = N_HEADS_A * HEAD_DIM
BQ_W = N_HEADS_B * HEAD_DIM
BKV_W = N_KV_B * HEAD_DIM
CQ_W = N_HEADS_C * HEAD_DIM
CKV_W = N_KV_C * HEAD_DIM

VMEM_LIMIT = 56 * 1024 * 1024
LANES = 128
ROW_TILE = 512
FF_CHUNK = 256
BAND = 128
FOX_TQ = 512
FOX_TK = 512
FOX_SUB = 64
FOX_DEPTH = 8
CUM_BLOCK = 512
PAGES_PER_STEP = 16
BLOCKS_PER_ITER = 4
BAND_DEPTH = 2
FOX_PAGE_CHUNK = 4
LOG2E = 1.4426950408889634


def _dot(a, b):
    return jnp.dot(a, b, preferred_element_type=F32)


def _dot_nt(a, b):
    return lax.dot_general(a, b, (((1,), (1,)), ((), ())), preferred_element_type=F32)


def _dot_exact(a, b):
    return jnp.dot(a, b, preferred_element_type=F32, precision=lax.Precision.HIGHEST)


def _rms(x, g):
    return x * lax.rsqrt(jnp.mean(x * x, axis=-1, keepdims=True) + RMS_EPS) * g


def _params(*sem):
    return pltpu.CompilerParams(dimension_semantics=sem, vmem_limit_bytes=VMEM_LIMIT)


def _resident(shape, index_map):
    return pl.BlockSpec(shape, index_map, pipeline_mode=pl.Buffered(1))


def _ffn_kernel(x_ref, g_ref, win_ref, wout_ref, o_ref, act_ref):
    d_ff = wout_ref.shape[0]
    x = x_ref[...]
    h = _rms(x, g_ref[...]).astype(BF16)
    for c in range(d_ff // FF_CHUNK):
        lo = c * FF_CHUNK
        a = _dot(h, win_ref[:, lo:lo + FF_CHUNK])
        b = _dot(h, win_ref[:, d_ff + lo:d_ff + lo + FF_CHUNK])
        act_ref[:, lo:lo + FF_CHUNK] = (a * jax.nn.sigmoid(a) * b).astype(BF16)
    o_ref[...] = x + 0.5 * _dot(act_ref[...], wout_ref[...])


def _ffn(x, g, w_in, w_out, layer):
    n, d = x.shape
    d_ff = w_out.shape[1]
    return pl.pallas_call(
        _ffn_kernel,
        grid=(n // ROW_TILE,),
        in_specs=[
            pl.BlockSpec((ROW_TILE, d), lambda i: (i, 0)),
            pl.BlockSpec((None, 1, d), lambda i: (layer, 0, 0)),
            _resident((None, d, 2 * d_ff), lambda i: (layer, 0, 0)),
            _resident((None, d_ff, d), lambda i: (layer, 0, 0)),
        ],
        out_specs=pl.BlockSpec((ROW_TILE, d), lambda i: (i, 0)),
        out_shape=jax.ShapeDtypeStruct((n, d), F32),
        scratch_shapes=[pltpu.VMEM((ROW_TILE, d_ff), BF16)],
        compiler_params=_params("parallel"),
        name="ffn",
    )(x, g, w_in, w_out)


def _ple_kernel(x_ref, p_ref, g_ref, wg_ref, wp_ref, o_ref):
    x = x_ref[...]
    h = _rms(x, g_ref[...]).astype(BF16)
    gate = jax.nn.sigmoid(_dot(h, wg_ref[...]))
    o_ref[...] = x + gate * _dot(p_ref[...].astype(BF16), wp_ref[...])


def _ple(x, p, g, w_gate, w_proj, layer):
    n, d = x.shape
    dp = p.shape[-1]
    return pl.pallas_call(
        _ple_kernel,
        grid=(n // ROW_TILE,),
        in_specs=[
            pl.BlockSpec((ROW_TILE, d), lambda i: (i, 0)),
            pl.BlockSpec((None, ROW_TILE, dp), lambda i: (layer, i, 0)),
            pl.BlockSpec((None, 1, d), lambda i: (layer, 0, 0)),
            _resident((None, d, d), lambda i: (layer, 0, 0)),
            _resident((None, dp, d), lambda i: (layer, 0, 0)),
        ],
        out_specs=pl.BlockSpec((ROW_TILE, d), lambda i: (i, 0)),
        out_shape=jax.ShapeDtypeStruct((n, d), F32),
        compiler_params=_params("parallel"),
        name="ple",
    )(x, p, g, w_gate, w_proj)


def _final_norm_kernel(x_ref, g_ref, o_ref):
    o_ref[...] = _rms(x_ref[...], g_ref[...])


def _final_norm(x, g):
    n, d = x.shape
    return pl.pallas_call(
        _final_norm_kernel,
        grid=(n // ROW_TILE,),
        in_specs=[pl.BlockSpec((ROW_TILE, d), lambda i: (i, 0)), pl.BlockSpec((1, d), lambda i: (0, 0))],
        out_specs=pl.BlockSpec((ROW_TILE, d), lambda i: (i, 0)),
        out_shape=jax.ShapeDtypeStruct((n, d), F32),
        compiler_params=_params("parallel"),
        name="final_norm",
    )(x, g)


def _out_proj_kernel(*refs):
    x_ref, part_refs, w_ref, o_ref = refs[0], refs[1:-2], refs[-2], refs[-1]
    acc = x_ref[...]
    lo = 0
    for p_ref in part_refs:
        width = p_ref.shape[1]
        acc = acc + _dot(p_ref[...].astype(BF16), w_ref[lo:lo + width, :])
        lo += width
    o_ref[...] = acc


def _out_proj(x, parts, w, layer):
    n, d = x.shape
    return pl.pallas_call(
        _out_proj_kernel,
        grid=(n // ROW_TILE,),
        in_specs=[pl.BlockSpec((ROW_TILE, d), lambda i: (i, 0))]
        + [pl.BlockSpec((ROW_TILE, p.shape[1]), lambda i: (i, 0)) for p in parts]
        + [_resident((None, w.shape[1], d), lambda i: (layer, 0, 0))],
        out_specs=pl.BlockSpec((ROW_TILE, d), lambda i: (i, 0)),
        out_shape=jax.ShapeDtypeStruct((n, d), F32),
        compiler_params=_params("parallel"),
        name="out_proj",
    )(x, *parts, w)


def _rope_rows(y, cos, sin_signed, first_half):
    width = y.shape[1]
    partner = jnp.where(first_half, pltpu.roll(y, width - HALF, axis=1), pltpu.roll(y, HALF, axis=1))
    return y * cos + partner * sin_signed


def _rope_store_t(z, cos_t, sin_t, out_ref, n_heads):
    for h in range(n_heads):
        x1 = z[h * HEAD_DIM:h * HEAD_DIM + HALF]
        x2 = z[h * HEAD_DIM + HALF:(h + 1) * HEAD_DIM]
        out_ref[h * HEAD_DIM:h * HEAD_DIM + HALF, :] = x1 * cos_t - x2 * sin_t
        out_ref[h * HEAD_DIM + HALF:(h + 1) * HEAD_DIM, :] = x2 * cos_t + x1 * sin_t


def _proj_ab_kernel(x_ref, g_ref, wr_ref, wt_ref, cos_ref, sin_ref, cos_t_ref, sin_t_ref,
                    qa_ref, ka_ref, va_ref, qb_ref, ka_t_ref, va_t_ref, kb_t_ref, vb_t_ref):
    h = _rms(x_ref[...], g_ref[...]).astype(BF16)
    rows = h.shape[0]
    reps = A_W // LANES
    cos = jnp.concatenate([cos_ref[...]] * reps, axis=1)
    sin = jnp.concatenate([sin_ref[...]] * reps, axis=1)
    lane = lax.broadcasted_iota(jnp.int32, (rows, A_W), 1)
    first_half = (lane & (HEAD_DIM - 1)) < HALF
    rope = functools.partial(_rope_rows, cos=cos, sin_signed=sin, first_half=first_half)
    qa_ref[...] = rope(_dot(h, wr_ref[:, 0:A_W])) * ATTN_SCALE
    ka_ref[...] = rope(_dot(h, wr_ref[:, A_W:2 * A_W]))
    va_ref[...] = _dot(h, wr_ref[:, 2 * A_W:3 * A_W])
    qb_ref[...] = rope(_dot(h, wr_ref[:, 3 * A_W:3 * A_W + BQ_W])) * ATTN_SCALE
    cos_t = cos_t_ref[...]
    sin_t = sin_t_ref[...]
    _rope_store_t(_dot_nt(wt_ref[0:A_W, :], h), cos_t, sin_t, ka_t_ref, N_HEADS_A)
    va_t_ref[...] = _dot_nt(wt_ref[A_W:2 * A_W, :], h)
    _rope_store_t(_dot_nt(wt_ref[2 * A_W:2 * A_W + BKV_W, :], h), cos_t, sin_t, kb_t_ref, N_KV_B)
    vb_t_ref[...] = _dot_nt(wt_ref[2 * A_W + BKV_W:2 * A_W + 2 * BKV_W, :], h)


def _feature_major_out(seq):
    per = seq // ROW_TILE
    return lambda w: pl.BlockSpec((None, w, ROW_TILE), lambda i: (i // per, 0, i % per))


def _proj_ab(x, g, w_rows, w_t, tables, layer, e, seq):
    n, d = x.shape
    slots = -(-n // seq)
    cos, sin, cos_t, sin_t = tables
    row = lambda w: pl.BlockSpec((ROW_TILE, w), lambda i: (i, 0))
    col = lambda w: pl.BlockSpec((w, ROW_TILE), lambda i: (0, i))
    col_out = _feature_major_out(seq)
    sds = jax.ShapeDtypeStruct
    return pl.pallas_call(
        _proj_ab_kernel,
        grid=(n // ROW_TILE,),
        in_specs=[
            row(d),
            pl.BlockSpec((None, 1, d), lambda i: (layer, 0, 0)),
            _resident((None, d, w_rows.shape[2]), lambda i: (e, 0, 0)),
            _resident((None, w_t.shape[1], d), lambda i: (e, 0, 0)),
            row(LANES), row(LANES), col(HALF), col(HALF),
        ],
        out_specs=[row(A_W), row(A_W), row(A_W), row(BQ_W),
                   col_out(A_W), col_out(A_W), col_out(BKV_W), col_out(BKV_W)],
        out_shape=[sds((n, A_W), F32)] * 3 + [sds((n, BQ_W), F32)]
        + [sds((slots, A_W, seq), F32)] * 2 + [sds((slots, BKV_W, seq), F32)] * 2,
        compiler_params=_params("parallel"),
        name="proj_ab",
    )(x, g, w_rows, w_t, cos, sin, cos_t, sin_t)


def _proj_c_kernel(x_ref, g_ref, wq_ref, wt_ref, bf_ref, q_ref, k_t_ref, v_t_ref, lf_t_ref):
    h = _rms(x_ref[...], g_ref[...]).astype(BF16)
    q_ref[...] = _dot(h, wq_ref[...]) * (ATTN_SCALE * LOG2E)
    k_t_ref[...] = _dot_nt(wt_ref[0:CKV_W, :], h)
    v_t_ref[...] = _dot_nt(wt_ref[CKV_W:2 * CKV_W, :], h)
    f_t = _dot_nt(wt_ref[2 * CKV_W:2 * CKV_W + N_HEADS_C, :], h)
    lf_t_ref[...] = jax.nn.log_sigmoid(f_t + bf_ref[...])


def _proj_c(x, g, w_q, w_t, b_forget, layer, o, seq):
    n, d = x.shape
    slots = -(-n // seq)
    row = lambda w: pl.BlockSpec((ROW_TILE, w), lambda i: (i, 0))
    col = _feature_major_out(seq)
    sds = jax.ShapeDtypeStruct
    return pl.pallas_call(
        _proj_c_kernel,
        grid=(n // ROW_TILE,),
        in_specs=[
            row(d),
            pl.BlockSpec((None, 1, d), lambda i: (layer, 0, 0)),
            _resident((None, d, CQ_W), lambda i: (o, 0, 0)),
            _resident((None, w_t.shape[1], d), lambda i: (o, 0, 0)),
            pl.BlockSpec((None, N_HEADS_C, 1), lambda i: (o, 0, 0)),
        ],
        out_specs=[row(CQ_W), col(CKV_W), col(CKV_W), col(N_HEADS_C)],
        out_shape=[sds((n, CQ_W), F32), sds((slots, CKV_W, seq), F32), sds((slots, CKV_W, seq), F32),
                   sds((slots, N_HEADS_C, seq), F32)],
        compiler_params=_params("parallel"),
        name="proj_c",
    )(x, g, w_q, w_t, b_forget)


def _software_pipeline(n, issue, consume, depth=1):
    pending = [issue(k) for k in range(min(depth, n))]
    for k in range(n):
        if k + depth < n:
            pending.append(issue(k + depth))
        consume(k, pending.pop(0))


def _softmax_2blk(sp, sc):
    m = jnp.max(jnp.maximum(sp, sc), axis=1, keepdims=True)
    return m, jnp.concatenate([jnp.exp(sp - m), jnp.exp(sc - m)], axis=1).astype(BF16)


def _prompt_a_kernel(q_ref, k_ref, v_ref, o_ref, num_ref, den_ref, m_ref):
    seq = q_ref.shape[0]
    ri = lax.broadcasted_iota(jnp.int32, (BAND, BAND), 0)
    ci = lax.broadcasted_iota(jnp.int32, (BAND, BAND), 1)
    prev_ok = ci >= ri
    cur_ok = ci <= ri
    low = lax.broadcasted_iota(jnp.int32, (BAND, LANES), 1) < HEAD_DIM
    ones = jnp.ones((2 * BAND, LANES), BF16)

    for pi, (window, r) in enumerate(A_PATTERNS):
        assert window // r == BAND
        nblk = seq // (r * BAND)

        def rows(start, r=r):
            return pl.ds(start, BAND) if r == 1 else pl.ds(start, BAND, stride=r)

        def body(it, carry, r=r, nblk=nblk, pi=pi, rows=rows):
            heads = LANES // HEAD_DIM
            blocks = {}

            def block(u):
                if u not in blocks:
                    idx = it * BLOCKS_PER_ITER + u
                    rho = idx // nblk
                    n = idx - rho * nblk
                    q_rows = rows(rho + r * BAND * n)
                    p_rows = rows(rho + r * BAND * jnp.maximum(n - 1, 0))
                    blocks[u] = dict(
                        q_rows=q_rows, q=q_ref[q_rows, :],
                        kc=k_ref[q_rows, :].astype(BF16), kp=k_ref[p_rows, :].astype(BF16),
                        vv=jnp.concatenate([v_ref[p_rows, :], v_ref[q_rows, :]], axis=0).astype(BF16),
                        no_prev=jnp.where(n > 0, 0.0, NEG), nums=[], dens=[], ms=[])
                return blocks[u]

            def logits(k):
                u, h = divmod(k, heads)
                blk = block(u)
                qh = jnp.where(low if h == 0 else jnp.logical_not(low), blk["q"], 0.0).astype(BF16)
                sp = jnp.where(prev_ok, _dot_nt(qh, blk["kp"]), NEG) + blk["no_prev"]
                sc = jnp.where(cur_ok, _dot_nt(qh, blk["kc"]), NEG)
                return sp, sc

            def finish_chain(k, s):
                u, h = divmod(k, heads)
                blk = block(u)
                m, p = _softmax_2blk(*s)
                blk["nums"].append(_dot(p, blk["vv"]))
                blk["dens"].append(_dot(p, ones))
                blk["ms"].append(m)
                if h + 1 < heads:
                    return
                q_rows = blk["q_rows"]
                num = jnp.where(low, blk["nums"][0], blk["nums"][1])
                den = jnp.where(low, blk["dens"][0], blk["dens"][1])
                m = jnp.where(low, blk["ms"][0], blk["ms"][1])
                if pi > 0:
                    m_old = m_ref[q_rows, :]
                    big = jnp.maximum(m_old, m)
                    a_old = jnp.exp(m_old - big)
                    a_new = jnp.exp(m - big)
                    num = a_old * num_ref[q_rows, :] + a_new * num
                    den = a_old * den_ref[q_rows, :] + a_new * den
                    m = big
                num_ref[q_rows, :] = num
                den_ref[q_rows, :] = den
                m_ref[q_rows, :] = m

            _software_pipeline(BLOCKS_PER_ITER * heads, logits, finish_chain, BAND_DEPTH)
            return carry

        assert (r * nblk) % BLOCKS_PER_ITER == 0
        lax.fori_loop(0, r * nblk // BLOCKS_PER_ITER, body, 0)

    def finish(i, carry):
        sl = pl.ds(pl.multiple_of(i * BAND, BAND), BAND)
        o_ref[sl, :] = num_ref[sl, :] / den_ref[sl, :]
        return carry

    lax.fori_loop(0, seq // BAND, finish, 0)


def _prompt_a(qa, ka, va, n_seq, seq):
    n = qa.shape[0]
    blk = pl.BlockSpec((seq, LANES), lambda b, hp: (b, hp))
    return pl.pallas_call(
        _prompt_a_kernel,
        grid=(n_seq, A_W // LANES),
        in_specs=[blk, blk, blk],
        out_specs=blk,
        out_shape=jax.ShapeDtypeStruct((n, A_W), F32),
        scratch_shapes=[pltpu.VMEM((seq, LANES), F32)] * 3,
        compiler_params=_params("parallel", "parallel"),
        name="prompt_a",
    )(qa, ka, va)


def _prompt_b_kernel(sink_ref, q_ref, k_t_ref, v_t_ref, o_ref):
    seq = q_ref.shape[0]
    g = pl.program_id(1)
    ri = lax.broadcasted_iota(jnp.int32, (BAND, BAND), 0)
    ci = lax.broadcasted_iota(jnp.int32, (BAND, BAND), 1)
    prev_ok = ci >= ri
    cur_ok = ci <= ri

    def body(it, carry):
        blocks = {}

        def block(u):
            if u not in blocks:
                n = it * BLOCKS_PER_ITER + u
                cur = pl.ds(pl.multiple_of(n * BAND, BAND), BAND)
                prev = pl.ds(pl.multiple_of(jnp.maximum(n - 1, 0) * BAND, BAND), BAND)
                vv = jnp.concatenate([v_t_ref[:, prev], v_t_ref[:, cur]], axis=1).astype(BF16)
                blocks[u] = dict(
                    cur=cur, q=q_ref[cur, :], kc=k_t_ref[:, cur].astype(BF16), kp=k_t_ref[:, prev].astype(BF16),
                    v_and_ones=jnp.concatenate([vv, jnp.ones_like(vv)], axis=0),
                    no_prev=jnp.where(n > 0, 0.0, NEG), outs=[])
            return blocks[u]

        def logits(k):
            u, hh = divmod(k, G_B)
            blk = block(u)
            qh = blk["q"][:, hh * HEAD_DIM:(hh + 1) * HEAD_DIM].astype(BF16)
            sp = jnp.where(prev_ok, _dot(qh, blk["kp"]), NEG) + blk["no_prev"]
            sc = jnp.where(cur_ok, _dot(qh, blk["kc"]), NEG)
            return sp, sc

        def finish_chain(k, s):
            u, hh = divmod(k, G_B)
            blk = block(u)
            m, p = _softmax_2blk(*s)
            num_den = _dot_nt(p, blk["v_and_ones"])
            num, den = num_den[:, :HEAD_DIM], num_den[:, HEAD_DIM:]
            sink = sink_ref[g * G_B + hh]
            big = jnp.maximum(m, sink)
            a = jnp.exp(m - big)
            blk["outs"].append(num * a / (den * a + jnp.exp(sink - big)))
            if hh + 1 == G_B:
                o_ref[blk["cur"], :] = jnp.concatenate(blk["outs"], axis=1)

        _software_pipeline(BLOCKS_PER_ITER * G_B, logits, finish_chain, BAND_DEPTH)
        return carry

    assert (seq // BAND) % BLOCKS_PER_ITER == 0
    lax.fori_loop(0, seq // BAND // BLOCKS_PER_ITER, body, 0)


def _prompt_b(sink, qb, kb_t, vb_t, n_seq, seq):
    n = qb.shape[0]
    gw = G_B * HEAD_DIM
    q_spec = pl.BlockSpec((seq, gw), lambda b, g: (b, g))
    kv_spec = pl.BlockSpec((None, HEAD_DIM, seq), lambda b, g: (b, g, 0))
    return pl.pallas_call(
        _prompt_b_kernel,
        grid=(n_seq, N_KV_B),
        in_specs=[pl.BlockSpec(memory_space=pltpu.SMEM), q_spec, kv_spec, kv_spec],
        out_specs=q_spec,
        out_shape=jax.ShapeDtypeStruct((n, BQ_W), F32),
        compiler_params=_params("parallel", "parallel"),
        name="prompt_b",
    )(sink, qb, kb_t, vb_t)


def _shifted_buffer(cache, new):
    return jnp.concatenate([cache[:, new.shape[1]:], new], axis=1)


def _sample_a_kernel(q_ref, ck_ref, cv_ref, nk_ref, nv_ref, *rest):
    o_ref, ok_ref, ov_ref = rest[-3:]
    t_new, buf = q_ref.shape[0], ck_ref.shape[1]
    q = q_ref[...].astype(BF16)
    kc, vc = ck_ref[...], cv_ref[...]
    kn, vn = nk_ref[...], nv_ref[...]

    def count(d):
        c = jnp.zeros(d.shape, F32)
        for window, r in A_PATTERNS:
            assert r & (r - 1) == 0
            c = c + jnp.where(((d & (r - 1)) == 0) & (d <= window) & (d >= 0), 1.0, 0.0)
        return c

    d_c = buf + lax.broadcasted_iota(jnp.int32, (t_new, buf), 0) - lax.broadcasted_iota(jnp.int32, (t_new, buf), 1)
    d_n = lax.broadcasted_iota(jnp.int32, (t_new, t_new), 0) - lax.broadcasted_iota(jnp.int32, (t_new, t_new), 1)
    w_c, w_n = count(d_c), count(d_n)
    s_c = jnp.where(w_c > 0, _dot(q, kc.astype(BF16)), NEG)
    s_n = jnp.where(w_n > 0, _dot(q, kn.astype(BF16)), NEG)
    m = jnp.maximum(jnp.max(s_c, axis=1, keepdims=True), jnp.max(s_n, axis=1, keepdims=True))
    e_c = jnp.exp(s_c - m) * w_c
    e_n = jnp.exp(s_n - m) * w_n
    den = jnp.sum(e_c, axis=1, keepdims=True) + jnp.sum(e_n, axis=1, keepdims=True)
    num = _dot_nt(e_c.astype(BF16), vc.astype(BF16)) + _dot_nt(e_n.astype(BF16), vn.astype(BF16))
    o_ref[...] = num / den
    ok_ref[...] = _shifted_buffer(kc, kn)
    ov_ref[...] = _shifted_buffer(vc, vn)


def _sample_a(q, cache_k_t, cache_v_t, new_k_t, new_v_t, e, prev_out):
    n_e, n_b, n_h, _, buf = cache_k_t.shape
    t_new = q.shape[2]
    q_spec = pl.BlockSpec((None, None, t_new, HEAD_DIM), lambda b, h: (b, h, 0, 0))
    c_spec = pl.BlockSpec((None, None, None, HEAD_DIM, buf), lambda b, h: (e, b, h, 0, 0))
    n_spec = pl.BlockSpec((None, None, HEAD_DIM, t_new), lambda b, h: (b, h, 0, 0))
    in_specs = [q_spec, c_spec, c_spec, n_spec, n_spec]
    args = [q, cache_k_t, cache_v_t, new_k_t, new_v_t]
    aliases = {}
    if prev_out is not None:
        in_specs += [pl.BlockSpec(memory_space=pl.ANY)] * 2
        args += list(prev_out)
        aliases = {5: 1, 6: 2}
    return pl.pallas_call(
        _sample_a_kernel,
        grid=(n_b, n_h),
        in_specs=in_specs,
        out_specs=[q_spec, c_spec, c_spec],
        out_shape=[jax.ShapeDtypeStruct(q.shape, F32),
                   jax.ShapeDtypeStruct(cache_k_t.shape, F32),
                   jax.ShapeDtypeStruct(cache_v_t.shape, F32)],
        input_output_aliases=aliases,
        compiler_params=_params("parallel", "parallel"),
        name="sample_a",
    )(*args)


def _sample_b_kernel(sink_ref, q_ref, ck_ref, cv_ref, nk_ref, nv_ref, *rest):
    o_ref, ok_ref, ov_ref = rest[-3:]
    rows, buf, t_new = q_ref.shape[1], ck_ref.shape[2], nk_ref.shape[2]
    t_c = lax.broadcasted_iota(jnp.int32, (rows, buf), 0) & (t_new - 1)
    d_c = buf + t_c - lax.broadcasted_iota(jnp.int32, (rows, buf), 1)
    t_n = lax.broadcasted_iota(jnp.int32, (rows, t_new), 0) & (t_new - 1)
    d_n = t_n - lax.broadcasted_iota(jnp.int32, (rows, t_new), 1)
    ok_c = d_c <= WINDOW_B
    ok_n = (d_n >= 0) & (d_n <= WINDOW_B)
    head = lax.broadcasted_iota(jnp.int32, (rows, 1), 0) // t_new
    for g in range(N_KV_B):
        q = q_ref[g].astype(BF16)
        kc, vc = ck_ref[g], cv_ref[g]
        kn, vn = nk_ref[g], nv_ref[g]
        s_c = jnp.where(ok_c, _dot(q, kc.astype(BF16)), NEG)
        s_n = jnp.where(ok_n, _dot(q, kn.astype(BF16)), NEG)
        m = jnp.maximum(jnp.max(s_c, axis=1, keepdims=True), jnp.max(s_n, axis=1, keepdims=True))
        e_c = jnp.exp(s_c - m)
        e_n = jnp.exp(s_n - m)
        den = jnp.sum(e_c, axis=1, keepdims=True) + jnp.sum(e_n, axis=1, keepdims=True)
        num = _dot_nt(e_c.astype(BF16), vc.astype(BF16)) + _dot_nt(e_n.astype(BF16), vn.astype(BF16))
        sink = jnp.zeros((rows, 1), F32)
        for hh in range(G_B):
            sink = jnp.where(head == hh, sink_ref[g * G_B + hh], sink)
        big = jnp.maximum(m, sink)
        a = jnp.exp(m - big)
        o_ref[g] = num * a / (den * a + jnp.exp(sink - big))
        ok_ref[g] = _shifted_buffer(kc, kn)
        ov_ref[g] = _shifted_buffer(vc, vn)


def _sample_b(sink, q, cache_k_t, cache_v_t, new_k_t, new_v_t, e, prev_out):
    n_e, n_b, n_kv, _, buf = cache_k_t.shape
    rows, t_new = q.shape[2], new_k_t.shape[3]
    assert t_new & (t_new - 1) == 0
    q_spec = pl.BlockSpec((None, n_kv, rows, HEAD_DIM), lambda b: (b, 0, 0, 0))
    c_spec = pl.BlockSpec((None, None, n_kv, HEAD_DIM, buf), lambda b: (e, b, 0, 0, 0))
    n_spec = pl.BlockSpec((None, n_kv, HEAD_DIM, t_new), lambda b: (b, 0, 0, 0))
    in_specs = [pl.BlockSpec(memory_space=pltpu.SMEM), q_spec, c_spec, c_spec, n_spec, n_spec]
    args = [sink, q, cache_k_t, cache_v_t, new_k_t, new_v_t]
    aliases = {}
    if prev_out is not None:
        in_specs += [pl.BlockSpec(memory_space=pl.ANY)] * 2
        args += list(prev_out)
        aliases = {6: 1, 7: 2}
    return pl.pallas_call(
        _sample_b_kernel,
        grid=(n_b,),
        in_specs=in_specs,
        out_specs=[q_spec, c_spec, c_spec],
        out_shape=[jax.ShapeDtypeStruct(q.shape, F32),
                   jax.ShapeDtypeStruct(cache_k_t.shape, F32),
                   jax.ShapeDtypeStruct(cache_v_t.shape, F32)],
        input_output_aliases=aliases,
        compiler_params=_params("parallel"),
        name="sample_b",
    )(*args)


def _upper_ones(n):
    return jnp.where(lax.broadcasted_iota(jnp.int32, (n, n), 0) <= lax.broadcasted_iota(jnp.int32, (n, n), 1),
                     1.0, 0.0).astype(F32)


def _cumsum_lanes(x, with_total=False):
    rows, n = x.shape
    hi = x.astype(BF16)
    rest = x - hi.astype(F32)
    mid = rest.astype(BF16)
    lo = (rest - mid.astype(F32)).astype(BF16)
    rhs = _upper_ones(n).astype(BF16)
    if with_total:
        rhs = jnp.concatenate([rhs, jnp.ones((n, n), BF16)], axis=1)
    parts = _dot(jnp.concatenate([hi, mid, lo], axis=0), rhs)
    both = parts[0:rows] + parts[rows:2 * rows] + parts[2 * rows:3 * rows]
    return (both[:, :n], both[:, n:]) if with_total else both


def _cum_kernel(lf_ref, o_ref, carry_ref):
    @pl.when(pl.program_id(1) == 0)
    def _():
        carry_ref[...] = jnp.zeros_like(carry_ref)

    width = lf_ref.shape[1]
    c = _cumsum_lanes(lf_ref[...]) + carry_ref[:, 0:1]
    o_ref[...] = c * LOG2E
    carry_ref[...] = jnp.broadcast_to(c[:, width - 1:width], carry_ref.shape)


def _prompt_cum(lf_t, n_seq, seq):
    heads = lf_t.shape[1]
    spec = pl.BlockSpec((None, heads, CUM_BLOCK), lambda b, j: (b, 0, j))
    return pl.pallas_call(
        _cum_kernel,
        grid=(n_seq, seq // CUM_BLOCK),
        in_specs=[spec],
        out_specs=spec,
        out_shape=jax.ShapeDtypeStruct((n_seq, heads, seq), F32),
        scratch_shapes=[pltpu.VMEM((heads, LANES), F32)],
        compiler_params=_params("parallel", "arbitrary"),
        name="prompt_cum",
    )(lf_t)


def _fox_prompt_kernel(q_ref, k_t_ref, v_t_ref, cum_ref, o_ref, m_ref, acc_ref):
    i, j = pl.program_id(1), pl.program_id(2)
    tq, tk = q_ref.shape[0], k_t_ref.shape[1]

    @pl.when(j == 0)
    def _():
        m_ref[...] = jnp.full(m_ref.shape, NEG, F32)
        acc_ref[...] = jnp.zeros_like(acc_ref)

    def step(masked):
        def body(r, carry):
            rows = pl.ds(pl.multiple_of(r * FOX_SUB, FOX_SUB), FOX_SUB)
            if masked:
                q_pos = i * tq + r * FOX_SUB + lax.broadcasted_iota(jnp.int32, (FOX_SUB, tk), 0)
                causal = j * tk + lax.broadcasted_iota(jnp.int32, (FOX_SUB, tk), 1) <= q_pos
            kv = {}

            def group(g):
                if g not in kv:
                    v_t = v_t_ref[g * HEAD_DIM:(g + 1) * HEAD_DIM, :].astype(BF16)
                    kv[g] = (k_t_ref[g * HEAD_DIM:(g + 1) * HEAD_DIM, :].astype(BF16),
                             jnp.concatenate([v_t, jnp.ones_like(v_t)], axis=0))
                return kv[g]

            def logits(h):
                qh = q_ref[rows, h * HEAD_DIM:(h + 1) * HEAD_DIM].astype(BF16)
                return _dot(qh, group(h // G_C)[0])

            def finish_chain(h, s):
                s = s - cum_ref[h:h + 1, :]
                if masked:
                    s = jnp.where(causal, s, NEG)
                m_old = m_ref[h, rows, :]
                m_new = jnp.maximum(m_old, jnp.max(s, axis=1, keepdims=True))
                p = jnp.exp2(s - m_new).astype(BF16)
                acc_ref[h, rows, :] = jnp.exp2(m_old - m_new) * acc_ref[h, rows, :] + _dot_nt(p, group(h // G_C)[1])
                m_ref[h, rows, :] = m_new

            _software_pipeline(N_HEADS_C, logits, finish_chain, FOX_DEPTH)
            return carry

        lax.fori_loop(0, tq // FOX_SUB, body, 0)

    all_visible = (j + 1) * tk - 1 <= i * tq
    some_visible = j * tk <= i * tq + tq - 1
    pl.when(all_visible)(lambda: step(False))
    pl.when(jnp.logical_and(some_visible, jnp.logical_not(all_visible)))(lambda: step(True))

    @pl.when(j == pl.num_programs(2) - 1)
    def _():
        o_ref[...] = jnp.concatenate(
            [acc_ref[h][:, :HEAD_DIM] / acc_ref[h][:, HEAD_DIM:] for h in range(N_HEADS_C)], axis=1)


def _fox_prompt(q, k_t, v_t, cum_t, n_seq, seq):
    n = q.shape[0]
    nq, nk = seq // FOX_TQ, seq // FOX_TK

    def kv_idx(b, i, j):
        return (b, 0, jnp.minimum(j, (i * FOX_TQ + FOX_TQ - 1) // FOX_TK))

    q_spec = pl.BlockSpec((FOX_TQ, CQ_W), lambda b, i, j: (b * nq + i, 0))
    return pl.pallas_call(
        _fox_prompt_kernel,
        grid=(n_seq, nq, nk),
        in_specs=[q_spec,
                  pl.BlockSpec((None, CKV_W, FOX_TK), kv_idx),
                  pl.BlockSpec((None, CKV_W, FOX_TK), kv_idx),
                  pl.BlockSpec((None, N_HEADS_C, FOX_TK), kv_idx)],
        out_specs=q_spec,
        out_shape=jax.ShapeDtypeStruct((n, CQ_W), F32),
        scratch_shapes=[pltpu.VMEM((N_HEADS_C, FOX_TQ, 1), F32),
                        pltpu.VMEM((N_HEADS_C, FOX_TQ, 2 * HEAD_DIM), F32)],
        compiler_params=_params("parallel", "parallel", "arbitrary"),
        name="fox_prompt",
    )(q, k_t, v_t, cum_t)


def _fox_sample_kernel(pt_ref, q_ref, *refs):
    npg = PAGES_PER_STEP
    k_pages, v_pages, lf_pages = refs[0:npg], refs[npg:2 * npg], refs[2 * npg:3 * npg]
    nk_ref, nv_ref, nlf_ref, o_ref, m_ref, l_ref, acc_ref, pref_ref = refs[3 * npg:]
    j = pl.program_id(1)
    page = lf_pages[0].shape[1]
    t_new = nk_ref.shape[2]
    rows = N_HEADS_C * t_new

    @pl.when(j == 0)
    def _():
        m_ref[...] = jnp.full(m_ref.shape, NEG, F32)
        l_ref[...] = jnp.zeros_like(l_ref)
        acc_ref[...] = jnp.zeros_like(acc_ref)
        pref_ref[...] = jnp.zeros_like(pref_ref)

    def per_row(c):
        return jnp.concatenate([c] * t_new, axis=0)

    def stacked(ref):
        return ref[...].reshape(CKV_W, ref.shape[2]).astype(BF16)

    def update(s_parts, v_parts):
        m_old = m_ref[...]
        m_new = m_old
        for s in s_parts:
            m_new = jnp.maximum(m_new, jnp.max(s, axis=1, keepdims=True))
        alpha = jnp.exp2(m_old - m_new)
        l_new = alpha * l_ref[...]
        acc = alpha * acc_ref[...]
        for s, v_t in zip(s_parts, v_parts):
            p = jnp.exp2(s - m_new)
            l_new = l_new + jnp.sum(p, axis=1, keepdims=True)
            acc = acc + _dot_nt(p.astype(BF16), v_t)
        l_ref[...] = l_new
        acc_ref[...] = acc
        m_ref[...] = m_new

    def chunk(refs, c):
        return jnp.concatenate([stacked(r) for r in refs[c * FOX_PAGE_CHUNK:(c + 1) * FOX_PAGE_CHUNK]], axis=1)

    q = q_ref[...]
    n_chunks = npg // FOX_PAGE_CHUNK
    qk = [_dot(q, chunk(k_pages, c)) for c in range(n_chunks)]
    local, total = _cumsum_lanes(jnp.concatenate([r[...] for r in lf_pages], axis=0), with_total=True)
    pref = pref_ref[...]
    bias = []
    for i in range(npg):
        sl = slice(i * N_HEADS_C, (i + 1) * N_HEADS_C)
        bias.append(per_row((local[sl] + pref) * LOG2E))
        pref = pref + total[sl]
    pref_ref[...] = pref
    update([qk[c] - jnp.concatenate(bias[c * FOX_PAGE_CHUNK:(c + 1) * FOX_PAGE_CHUNK], axis=1)
            for c in range(n_chunks)], [chunk(v_pages, c) for c in range(n_chunks)])

    @pl.when(j == pl.num_programs(1) - 1)
    def _():
        lf = nlf_ref[...]
        lane = lax.broadcasted_iota(jnp.int32, lf.shape, 1)
        cum_n = pref[:, :t_new]
        for t in range(t_new):
            cum_n = cum_n + jnp.where(lane >= t, lf[:, t:t + 1], 0.0)
        tok = lax.broadcasted_iota(jnp.int32, (rows, t_new), 0) // N_HEADS_C
        causal = lax.broadcasted_iota(jnp.int32, (rows, t_new), 1) <= tok
        s = _dot(q, stacked(nk_ref)) - per_row(cum_n * LOG2E)
        update([jnp.where(causal, s, NEG)], [stacked(nv_ref)])
        out = acc_ref[...] / l_ref[...]
        kv_head = (lax.broadcasted_iota(jnp.int32, (rows, HEAD_DIM), 0) % N_HEADS_C) // G_C
        o = out[:, 0:HEAD_DIM]
        for g in range(1, N_KV_C):
            o = jnp.where(kv_head == g, out[:, g * HEAD_DIM:(g + 1) * HEAD_DIM], o)
        o_ref[...] = o


def _fox_sample(page_table, q, cache_k_t, cache_v_t, cache_lf_t, new_k_t, new_v_t, new_lf_t, o):
    n_b, rows, _ = q.shape
    page = cache_k_t.shape[-1]
    t_new = new_k_t.shape[-1]
    n_pages = page_table.shape[1]
    npg = PAGES_PER_STEP
    assert t_new & (t_new - 1) == 0 and n_pages % npg == 0

    def page_spec(shape, i):
        return pl.BlockSpec((None, None) + shape, lambda b, j, pt: (o, pt[b, j * npg + i]) + (0,) * len(shape))

    q_spec = pl.BlockSpec((None, rows, CKV_W), lambda b, j, pt: (b, 0, 0))
    o_spec = pl.BlockSpec((None, rows, HEAD_DIM), lambda b, j, pt: (b, 0, 0))
    in_specs = [q_spec]
    in_specs += [page_spec((N_KV_C, HEAD_DIM, page), i) for i in range(npg)]
    in_specs += [page_spec((N_KV_C, HEAD_DIM, page), i) for i in range(npg)]
    in_specs += [page_spec((N_HEADS_C, page), i) for i in range(npg)]
    in_specs += [pl.BlockSpec((None, N_KV_C, HEAD_DIM, t_new), lambda b, j, pt: (b, 0, 0, 0))] * 2
    in_specs += [pl.BlockSpec((None, N_HEADS_C, t_new), lambda b, j, pt: (b, 0, 0))]
    grid_spec = pltpu.PrefetchScalarGridSpec(
        num_scalar_prefetch=1,
        grid=(n_b, n_pages // npg),
        in_specs=in_specs,
        out_specs=o_spec,
        scratch_shapes=[pltpu.VMEM((rows, 1), F32), pltpu.VMEM((rows, 1), F32),
                        pltpu.VMEM((rows, CKV_W), F32), pltpu.VMEM((N_HEADS_C, LANES), F32)],
    )
    return pl.pallas_call(
        _fox_sample_kernel,
        grid_spec=grid_spec,
        out_shape=jax.ShapeDtypeStruct((n_b, rows, HEAD_DIM), F32),
        compiler_params=_params("parallel", "arbitrary"),
        name="fox_sample",
    )(page_table, q, *([cache_k_t] * npg), *([cache_v_t] * npg), *([cache_lf_t] * npg),
      new_k_t, new_v_t, new_lf_t)


def _rope_tables(seq, n_sample_seq, t_new, past_len):
    inv_freq = ROPE_THETA ** (-jnp.arange(HALF, dtype=F32) / HALF)

    def tables(pos):
        ang = pos.astype(F32)[:, None] * inv_freq[None, :]
        return jnp.cos(ang), jnp.sin(ang)

    cos_p, sin_p = tables(jnp.arange(seq))
    cos_s, sin_s = tables(past_len + jnp.arange(t_new))
    return cos_p, sin_p, cos_s, sin_s


def kernel(x_prompt, x_sample, cache_a_k, cache_a_v, cache_b_k, cache_b_v, cache_c_k, cache_c_v, cache_c_logf, page_table, p_prompt, p_sample, g_ffn1, w_ffn1_in, w_ffn1_out, g_mix, w_in_ab, w_out_ab, sink_b, w_in_c, b_forget, w_out_c, g_ffn2, w_ffn2_in, w_ffn2_out, g_ple, w_ple_gate, w_ple_proj, g_final):
    n_seq, seq, d = x_prompt.shape
    n_dec, t_new, _ = x_sample.shape
    depth = g_ffn1.shape[0]
    n_p, n_s = n_seq * seq, n_dec * t_new
    n = n_p + n_s
    past_len = page_table.shape[1] * cache_c_k.shape[2]
    assert n_p % ROW_TILE == 0 and n_s % ROW_TILE == 0 and seq % ROW_TILE == 0

    x = jnp.concatenate([x_prompt.reshape(n_p, d), x_sample.reshape(n_s, d)], axis=0)
    p_all = jnp.concatenate([p_prompt.reshape(depth, n_p, -1), p_sample.reshape(depth, n_s, -1)], axis=1)

    cos_p, sin_p, cos_s, sin_s = _rope_tables(seq, n_dec, t_new, past_len)
    cos = jnp.concatenate([jnp.tile(cos_p, (n_seq, 1)), jnp.tile(cos_s, (n_dec, 1))], axis=0)
    sin = jnp.concatenate([jnp.tile(sin_p, (n_seq, 1)), jnp.tile(sin_s, (n_dec, 1))], axis=0)
    reps = LANES // HEAD_DIM
    tables = (jnp.tile(cos, (1, 2 * reps)), jnp.tile(jnp.concatenate([-sin, sin], axis=1), (1, reps)), cos.T, sin.T)

    bf = lambda w: w.astype(BF16)
    g3 = lambda g: g.reshape(depth, 1, d)
    g_ffn1, g_mix, g_ffn2, g_ple = g3(g_ffn1), g3(g_mix), g3(g_ffn2), g3(g_ple)
    w_ffn1_in, w_ffn1_out, w_ffn2_in, w_ffn2_out = bf(w_ffn1_in), bf(w_ffn1_out), bf(w_ffn2_in), bf(w_ffn2_out)
    w_ple_gate, w_ple_proj, w_out_ab, w_out_c = bf(w_ple_gate), bf(w_ple_proj), bf(w_out_ab), bf(w_out_c)
    qb_lo, kb_lo = 3 * A_W, 3 * A_W + BQ_W
    w_ab_rows = bf(w_in_ab[:, :, :kb_lo])
    w_ab_t = bf(jnp.swapaxes(jnp.concatenate([w_in_ab[:, :, A_W:3 * A_W], w_in_ab[:, :, kb_lo:]], axis=2), 1, 2))
    w_c_q = bf(w_in_c[:, :, :CQ_W])
    w_c_t = bf(jnp.swapaxes(w_in_c[:, :, CQ_W:], 1, 2))
    b_forget3 = b_forget.reshape(b_forget.shape[0], N_HEADS_C, 1)

    tr = lambda c: jnp.transpose(c, (0, 1, 3, 4, 2))
    cache_a_k_t, cache_a_v_t, cache_b_k_t, cache_b_v_t = tr(cache_a_k), tr(cache_a_v), tr(cache_b_k), tr(cache_b_v)
    cache_c_k_t, cache_c_v_t = tr(cache_c_k), tr(cache_c_v)
    cache_c_lf_t = jnp.transpose(cache_c_logf, (0, 1, 3, 2))

    assert n_s <= seq

    def sample_cols(a_t, heads):
        return a_t[n_seq, :, :n_s].reshape(heads, HEAD_DIM, n_dec, t_new).transpose(2, 0, 1, 3)

    def prompt_cols(a_t, heads, keep):
        a = a_t[:n_seq, :, seq - keep:].reshape(n_seq, heads, HEAD_DIM, keep)
        return a.transpose(0, 3, 1, 2)

    kv_of_head = jnp.repeat(jnp.eye(N_KV_C, dtype=F32), G_C, axis=0)

    buf_a, buf_b = cache_a_k.shape[2], cache_b_k.shape[2]
    keep_a, keep_b = min(buf_a, seq), min(buf_b, seq)
    ak_p, av_p, bk_p, bv_p = [], [], [], []
    ck_p, cv_p, cl_p, ck_s, cv_s, cl_s = [], [], [], [], [], []
    a_out, b_out = None, None

    for l in range(depth):
        x = _ffn(x, g_ffn1, w_ffn1_in, w_ffn1_out, l)
        if l % 2 == 0:
            e = l // 2
            qa, ka, va, qb, ka_t, va_t, kb_t, vb_t = _proj_ab(x, g_mix, w_ab_rows, w_ab_t, tables, l, e, seq)
            oa = _prompt_a(qa, ka, va, n_seq, seq)
            ob = _prompt_b(sink_b[e], qb, kb_t, vb_t, n_seq, seq)
            qa_s = qa[n_p:].reshape(n_dec, t_new, N_HEADS_A, HEAD_DIM).transpose(0, 2, 1, 3)
            a_res = _sample_a(qa_s, cache_a_k_t, cache_a_v_t, sample_cols(ka_t, N_HEADS_A),
                              sample_cols(va_t, N_HEADS_A), e, a_out)
            oa_s, a_out = a_res[0], a_res[1:]
            qb_s = qb[n_p:].reshape(n_dec, t_new, N_KV_B, G_B, HEAD_DIM).transpose(0, 2, 3, 1, 4)
            qb_s = qb_s.reshape(n_dec, N_KV_B, G_B * t_new, HEAD_DIM)
            b_res = _sample_b(sink_b[e], qb_s, cache_b_k_t, cache_b_v_t, sample_cols(kb_t, N_KV_B),
                              sample_cols(vb_t, N_KV_B), e, b_out)
            ob_s, b_out = b_res[0], b_res[1:]
            oa_s = oa_s.transpose(0, 2, 1, 3).reshape(n_s, A_W)
            ob_s = ob_s.reshape(n_dec, N_KV_B, G_B, t_new, HEAD_DIM).transpose(0, 3, 1, 2, 4).reshape(n_s, BQ_W)
            oa = lax.dynamic_update_slice(oa, oa_s, (n_p, 0))
            ob = lax.dynamic_update_slice(ob, ob_s, (n_p, 0))
            x = _out_proj(x, [oa, ob], w_out_ab, e)
            ak_p.append(prompt_cols(ka_t, N_HEADS_A, keep_a))
            av_p.append(prompt_cols(va_t, N_HEADS_A, keep_a))
            bk_p.append(prompt_cols(kb_t, N_KV_B, keep_b))
            bv_p.append(prompt_cols(vb_t, N_KV_B, keep_b))
        else:
            o = l // 2
            qc, kc_t, vc_t, lf_t = _proj_c(x, g_mix, w_c_q, w_c_t, b_forget3, l, o, seq)
            cum_t = _prompt_cum(lf_t, n_seq, seq)
            oc = _fox_prompt(qc, kc_t, vc_t, cum_t, n_seq, seq)
            qc_s = qc[n_p:].reshape(n_dec, t_new, N_HEADS_C, 1, HEAD_DIM)
            qc_s = (qc_s * kv_of_head[None, None, :, :, None]).astype(BF16)
            qc_s = qc_s.reshape(n_dec, t_new * N_HEADS_C, CKV_W)
            lf_s = lf_t[n_seq, :, :n_s].reshape(N_HEADS_C, n_dec, t_new)
            new_lf_t = lf_s.transpose(1, 0, 2)
            oc_s = _fox_sample(page_table, qc_s, cache_c_k_t, cache_c_v_t, cache_c_lf_t,
                               sample_cols(kc_t, N_KV_C), sample_cols(vc_t, N_KV_C), new_lf_t, o)
            oc = lax.dynamic_update_slice(oc, oc_s.reshape(n_s, CQ_W), (n_p, 0))
            x = _out_proj(x, [oc], w_out_c, o)
            ck_p.append(prompt_cols(kc_t, N_KV_C, seq))
            cv_p.append(prompt_cols(vc_t, N_KV_C, seq))
            cl_p.append(lf_t[:n_seq].transpose(0, 2, 1))
            ck_s.append(sample_cols(kc_t, N_KV_C).transpose(0, 3, 1, 2))
            cv_s.append(sample_cols(vc_t, N_KV_C).transpose(0, 3, 1, 2))
            cl_s.append(lf_s.transpose(1, 2, 0))
        x = _ffn(x, g_ffn2, w_ffn2_in, w_ffn2_out, l)
        x = _ple(x, p_all, g_ple, w_ple_gate, w_ple_proj, l)

    y = _final_norm(x, g_final.reshape(1, d))
    back = lambda c: jnp.transpose(c, (0, 1, 4, 2, 3))
    return (y[:n_p].reshape(n_seq, seq, d), y[n_p:].reshape(n_dec, t_new, d),
            jnp.stack(ak_p), jnp.stack(av_p), back(a_out[0]), back(a_out[1]),
            jnp.stack(bk_p), jnp.stack(bv_p), back(b_out[0]), back(b_out[1]),
            jnp.stack(ck_p), jnp.stack(cv_p), jnp.stack(cl_p),
            jnp.stack(ck_s), jnp.stack(cv_s), jnp.stack(cl_s))
```

```python
import functools

import jax
import jax.numpy as jnp
from jax import lax
from jax.experimental import pallas as pl
from jax.experimental.pallas import tpu as pltpu

F32 = jnp.float32
BF16 = jnp.bfloat16

HEAD_DIM = 64
HALF = HEAD_DIM // 2
ATTN_SCALE = HEAD_DIM ** -0.5
A_PATTERNS = ((128, 1), (512, 4), (2048, 16))
N_HEADS_A = 8
N_HEADS_B = 8
N_KV_B = 2
G_B = N_HEADS_B // N_KV_B
WINDOW_B = 128
N_HEADS_C = 16
N_KV_C = 4
G_C = N_HEADS_C // N_KV_C
ROPE_THETA = 10000.0
RMS_EPS = 1e-6
NEG = -1e30

A_W = N_HEADS_A * HEAD_DIM
BQ_W = N_HEADS_B * HEAD_DIM
BKV_W = N_KV_B * HEAD_DIM
CQ_W = N_HEADS_C * HEAD_DIM
CKV_W = N_KV_C * HEAD_DIM

VMEM_LIMIT = 56 * 1024 * 1024
LANES = 128
ROW_TILE = 512
FF_CHUNK = 256
BAND = 128
FOX_TQ = 512
FOX_TK = 512
FOX_SUB = 256
FOX_DEPTH = 15
CUM_BLOCK = 512
PAGES_PER_STEP = 16
BLOCKS_PER_ITER = 4
BAND_DEPTH = 2
FOX_PAGE_CHUNK = 4
LOG2E = 1.4426950408889634


def _dot(a, b):
    return jnp.dot(a, b, preferred_element_type=F32)


def _dot_nt(a, b):
    return lax.dot_general(a, b, (((1,), (1,)), ((), ())), preferred_element_type=F32)


def _dot_exact(a, b):
    return jnp.dot(a, b, preferred_element_type=F32, precision=lax.Precision.HIGHEST)


def _rms(x, g):
    return x * lax.rsqrt(jnp.mean(x * x, axis=-1, keepdims=True) + RMS_EPS) * g


def _params(*sem):
    return pltpu.CompilerParams(dimension_semantics=sem, vmem_limit_bytes=VMEM_LIMIT)


def _resident(shape, index_map):
    return pl.BlockSpec(shape, index_map, pipeline_mode=pl.Buffered(1))


def _split_row_specs(n_first, width, lead=()):
    tiles = n_first // ROW_TILE
    pad = (None,) * len(lead)
    return [pl.BlockSpec(pad + (ROW_TILE, width), lambda i: lead + (jnp.minimum(i, tiles - 1), 0)),
            pl.BlockSpec(pad + (ROW_TILE, width), lambda i: lead + (jnp.maximum(i - tiles, 0), 0))]


def _ffn_kernel(*refs, first_tiles):
    x_refs, (g_ref, win_ref, wout_ref, o_ref, act_ref) = refs[:-5], refs[-5:]
    d_ff = wout_ref.shape[0]
    x = x_refs[0][...]
    if len(x_refs) == 2:
        x = jnp.where(pl.program_id(0) < first_tiles, x, x_refs[1][...])
    h = _rms(x, g_ref[...]).astype(BF16)
    for c in range(d_ff // FF_CHUNK):
        lo = c * FF_CHUNK
        a = _dot(h, win_ref[:, lo:lo + FF_CHUNK])
        b = _dot(h, win_ref[:, d_ff + lo:d_ff + lo + FF_CHUNK])
        act_ref[:, lo:lo + FF_CHUNK] = (a * jax.nn.sigmoid(a) * b).astype(BF16)
    o_ref[...] = x + 0.5 * _dot(act_ref[...], wout_ref[...])


def _ffn(xs, g, w_in, w_out, layer):
    n, d = sum(x.shape[0] for x in xs), xs[0].shape[1]
    d_ff = w_out.shape[1]
    x_specs = [pl.BlockSpec((ROW_TILE, d), lambda i: (i, 0))] if len(xs) == 1 else _split_row_specs(xs[0].shape[0], d)
    return pl.pallas_call(
        functools.partial(_ffn_kernel, first_tiles=xs[0].shape[0] // ROW_TILE),
        grid=(n // ROW_TILE,),
        in_specs=x_specs + [
            pl.BlockSpec((None, 1, d), lambda i: (layer, 0, 0)),
            _resident((None, d, 2 * d_ff), lambda i: (layer, 0, 0)),
            _resident((None, d_ff, d), lambda i: (layer, 0, 0)),
        ],
        out_specs=pl.BlockSpec((ROW_TILE, d), lambda i: (i, 0)),
        out_shape=jax.ShapeDtypeStruct((n, d), F32),
        scratch_shapes=[pltpu.VMEM((ROW_TILE, d_ff), BF16)],
        compiler_params=_params("parallel"),
        name="ffn",
    )(*xs, g, w_in, w_out)


def _ple_kernel(x_ref, p_first_ref, p_second_ref, g_ref, wg_ref, wp_ref, o_ref, *, first_tiles):
    x = x_ref[...]
    h = _rms(x, g_ref[...]).astype(BF16)
    gate = jax.nn.sigmoid(_dot(h, wg_ref[...]))
    p = jnp.where(pl.program_id(0) < first_tiles, p_first_ref[...], p_second_ref[...])
    o_ref[...] = x + gate * _dot(p.astype(BF16), wp_ref[...])


def _ple(x, p_first, p_second, g, w_gate, w_proj, layer):
    n, d = x.shape
    dp = p_first.shape[-1]
    return pl.pallas_call(
        functools.partial(_ple_kernel, first_tiles=p_first.shape[1] // ROW_TILE),
        grid=(n // ROW_TILE,),
        in_specs=[pl.BlockSpec((ROW_TILE, d), lambda i: (i, 0))]
        + _split_row_specs(p_first.shape[1], dp, lead=(layer,)) + [
            pl.BlockSpec((None, 1, d), lambda i: (layer, 0, 0)),
            _resident((None, d, d), lambda i: (layer, 0, 0)),
            _resident((None, dp, d), lambda i: (layer, 0, 0)),
        ],
        out_specs=pl.BlockSpec((ROW_TILE, d), lambda i: (i, 0)),
        out_shape=jax.ShapeDtypeStruct((n, d), F32),
        compiler_params=_params("parallel"),
        name="ple",
    )(x, p_first, p_second, g, w_gate, w_proj)


def _final_norm_kernel(x_ref, g_ref, o_ref):
    o_ref[...] = _rms(x_ref[...], g_ref[...])


def _final_norm(x, g):
    n, d = x.shape
    return pl.pallas_call(
        _final_norm_kernel,
        grid=(n // ROW_TILE,),
        in_specs=[pl.BlockSpec((ROW_TILE, d), lambda i: (i, 0)), pl.BlockSpec((1, d), lambda i: (0, 0))],
        out_specs=pl.BlockSpec((ROW_TILE, d), lambda i: (i, 0)),
        out_shape=jax.ShapeDtypeStruct((n, d), F32),
        compiler_params=_params("parallel"),
        name="final_norm",
    )(x, g)


def _out_proj_kernel(*refs):
    x_ref, part_refs, w_ref, o_ref = refs[0], refs[1:-2], refs[-2], refs[-1]
    acc = x_ref[...]
    lo = 0
    for p_ref in part_refs:
        width = p_ref.shape[1]
        acc = acc + _dot(p_ref[...].astype(BF16), w_ref[lo:lo + width, :])
        lo += width
    o_ref[...] = acc


def _out_proj(x, parts, w, layer):
    n, d = x.shape
    return pl.pallas_call(
        _out_proj_kernel,
        grid=(n // ROW_TILE,),
        in_specs=[pl.BlockSpec((ROW_TILE, d), lambda i: (i, 0))]
        + [pl.BlockSpec((ROW_TILE, p.shape[1]), lambda i: (i, 0)) for p in parts]
        + [_resident((None, w.shape[1], d), lambda i: (layer, 0, 0))],
        out_specs=pl.BlockSpec((ROW_TILE, d), lambda i: (i, 0)),
        out_shape=jax.ShapeDtypeStruct((n, d), F32),
        compiler_params=_params("parallel"),
        name="out_proj",
    )(x, *parts, w)


def _rope_rows(y, cos, sin_signed, first_half):
    width = y.shape[1]
    partner = jnp.where(first_half, pltpu.roll(y, width - HALF, axis=1), pltpu.roll(y, HALF, axis=1))
    return y * cos + partner * sin_signed


def _rope_store_t(z, cos_t, sin_t, out_ref, n_heads):
    for h in range(n_heads):
        x1 = z[h * HEAD_DIM:h * HEAD_DIM + HALF]
        x2 = z[h * HEAD_DIM + HALF:(h + 1) * HEAD_DIM]
        out_ref[h * HEAD_DIM:h * HEAD_DIM + HALF, :] = x1 * cos_t - x2 * sin_t
        out_ref[h * HEAD_DIM + HALF:(h + 1) * HEAD_DIM, :] = x2 * cos_t + x1 * sin_t


def _proj_ab_kernel(x_ref, g_ref, wr_ref, wt_ref, cos_ref, sin_ref, cos_t_ref, sin_t_ref,
                    qa_ref, ka_ref, va_ref, qb_ref, ka_t_ref, va_t_ref, kb_t_ref, vb_t_ref):
    h = _rms(x_ref[...], g_ref[...]).astype(BF16)
    rows = h.shape[0]
    reps = A_W // LANES
    cos = jnp.concatenate([cos_ref[...]] * reps, axis=1)
    sin = jnp.concatenate([sin_ref[...]] * reps, axis=1)
    lane = lax.broadcasted_iota(jnp.int32, (rows, A_W), 1)
    first_half = (lane & (HEAD_DIM - 1)) < HALF
    rope = functools.partial(_rope_rows, cos=cos, sin_signed=sin, first_half=first_half)
    qa_ref[...] = rope(_dot(h, wr_ref[:, 0:A_W])) * ATTN_SCALE
    ka_ref[...] = rope(_dot(h, wr_ref[:, A_W:2 * A_W]))
    va_ref[...] = _dot(h, wr_ref[:, 2 * A_W:3 * A_W])
    qb_ref[...] = rope(_dot(h, wr_ref[:, 3 * A_W:3 * A_W + BQ_W])) * ATTN_SCALE
    cos_t = cos_t_ref[...]
    sin_t = sin_t_ref[...]
    _rope_store_t(_dot_nt(wt_ref[0:A_W, :], h), cos_t, sin_t, ka_t_ref, N_HEADS_A)
    va_t_ref[...] = _dot_nt(wt_ref[A_W:2 * A_W, :], h)
    _rope_store_t(_dot_nt(wt_ref[2 * A_W:2 * A_W + BKV_W, :], h), cos_t, sin_t, kb_t_ref, N_KV_B)
    vb_t_ref[...] = _dot_nt(wt_ref[2 * A_W + BKV_W:2 * A_W + 2 * BKV_W, :], h)


def _feature_major_out(seq):
    per = seq // ROW_TILE
    return lambda w: pl.BlockSpec((None, w, ROW_TILE), lambda i: (i // per, 0, i % per))


def _proj_ab(x, g, w_rows, w_t, tables, layer, e, seq):
    n, d = x.shape
    slots = -(-n // seq)
    cos, sin, cos_t, sin_t = tables
    row = lambda w: pl.BlockSpec((ROW_TILE, w), lambda i: (i, 0))
    col = lambda w: pl.BlockSpec((w, ROW_TILE), lambda i: (0, i))
    col_out = _feature_major_out(seq)
    sds = jax.ShapeDtypeStruct
    return pl.pallas_call(
        _proj_ab_kernel,
        grid=(n // ROW_TILE,),
        in_specs=[
            row(d),
            pl.BlockSpec((None, 1, d), lambda i: (layer, 0, 0)),
            _resident((None, d, w_rows.shape[2]), lambda i: (e, 0, 0)),
            _resident((None, w_t.shape[1], d), lambda i: (e, 0, 0)),
            row(LANES), row(LANES), col(HALF), col(HALF),
        ],
        out_specs=[row(A_W), row(A_W), row(A_W), row(BQ_W),
                   col_out(A_W), col_out(A_W), col_out(BKV_W), col_out(BKV_W)],
        out_shape=[sds((n, A_W), F32)] * 3 + [sds((n, BQ_W), F32)]
        + [sds((slots, A_W, seq), F32)] * 2 + [sds((slots, BKV_W, seq), F32)] * 2,
        compiler_params=_params("parallel"),
        name="proj_ab",
    )(x, g, w_rows, w_t, cos, sin, cos_t, sin_t)


def _proj_c_kernel(x_ref, g_ref, wq_ref, wt_ref, bf_ref, q_ref, k_t_ref, v_t_ref, lf_t_ref):
    h = _rms(x_ref[...], g_ref[...]).astype(BF16)
    q_ref[...] = _dot(h, wq_ref[...]) * (ATTN_SCALE * LOG2E)
    k_t_ref[...] = _dot_nt(wt_ref[0:CKV_W, :], h)
    v_t_ref[...] = _dot_nt(wt_ref[CKV_W:2 * CKV_W, :], h)
    f_t = _dot_nt(wt_ref[2 * CKV_W:2 * CKV_W + N_HEADS_C, :], h)
    lf_t_ref[...] = jax.nn.log_sigmoid(f_t + bf_ref[...])


def _proj_c(x, g, w_q, w_t, b_forget, layer, o, seq):
    n, d = x.shape
    slots = -(-n // seq)
    row = lambda w: pl.BlockSpec((ROW_TILE, w), lambda i: (i, 0))
    col = _feature_major_out(seq)
    sds = jax.ShapeDtypeStruct
    return pl.pallas_call(
        _proj_c_kernel,
        grid=(n // ROW_TILE,),
        in_specs=[
            row(d),
            pl.BlockSpec((None, 1, d), lambda i: (layer, 0, 0)),
            _resident((None, d, CQ_W), lambda i: (o, 0, 0)),
            _resident((None, w_t.shape[1], d), lambda i: (o, 0, 0)),
            pl.BlockSpec((None, N_HEADS_C, 1), lambda i: (o, 0, 0)),
        ],
        out_specs=[row(CQ_W), col(CKV_W), col(CKV_W), col(N_HEADS_C)],
        out_shape=[sds((n, CQ_W), F32), sds((slots, CKV_W, seq), F32), sds((slots, CKV_W, seq), F32),
                   sds((slots, N_HEADS_C, seq), F32)],
        compiler_params=_params("parallel"),
        name="proj_c",
    )(x, g, w_q, w_t, b_forget)


def _software_pipeline(n, issue, consume, depth=1):
    pending = [issue(k) for k in range(min(depth, n))]
    for k in range(n):
        if k + depth < n:
            pending.append(issue(k + depth))
        consume(k, pending.pop(0))


def _softmax_2blk(sp, sc):
    m = jnp.max(jnp.maximum(sp, sc), axis=1, keepdims=True)
    return m, jnp.concatenate([jnp.exp(sp - m), jnp.exp(sc - m)], axis=1).astype(BF16)


def _prompt_a_kernel(q_ref, k_ref, v_ref, o_ref, num_ref, den_ref, m_ref):
    seq = q_ref.shape[0]
    assert BAND & (BAND - 1) == 0
    ri = lax.broadcasted_iota(jnp.int32, (2 * BAND, BAND), 0) & (BAND - 1)
    ci = lax.broadcasted_iota(jnp.int32, (2 * BAND, BAND), 1)
    prev_ok2 = ci >= ri
    cur_ok2 = ci <= ri
    low = lax.broadcasted_iota(jnp.int32, (BAND, LANES), 1) < HEAD_DIM
    ones = jnp.ones((2 * BAND, LANES), BF16)

    for pi, (window, r) in enumerate(A_PATTERNS):
        assert window // r == BAND
        nblk = seq // (r * BAND)

        def rows(start, r=r):
            return pl.ds(start, BAND) if r == 1 else pl.ds(start, BAND, stride=r)

        def body(it, carry, r=r, nblk=nblk, pi=pi, rows=rows):
            heads = LANES // HEAD_DIM
            blocks = {}

            def block(u):
                if u not in blocks:
                    idx = it * BLOCKS_PER_ITER + u
                    rho = idx // nblk
                    n = idx - rho * nblk
                    q_rows = rows(rho + r * BAND * n)
                    p_rows = rows(rho + r * BAND * jnp.maximum(n - 1, 0))
                    blocks[u] = dict(
                        q_rows=q_rows, q=q_ref[q_rows, :],
                        kc=k_ref[q_rows, :].astype(BF16), kp=k_ref[p_rows, :].astype(BF16),
                        vv=jnp.concatenate([v_ref[p_rows, :], v_ref[q_rows, :]], axis=0).astype(BF16),
                        no_prev=jnp.where(n > 0, 0.0, NEG), nums=[], dens=[], ms=[])
                return blocks[u]

            def logits(u):
                blk = block(u)
                q = blk["q"]
                q2 = jnp.concatenate([jnp.where(low, q, 0.0), jnp.where(low, 0.0, q)], axis=0).astype(BF16)
                sp = jnp.where(prev_ok2, _dot_nt(q2, blk["kp"]), NEG) + blk["no_prev"]
                sc = jnp.where(cur_ok2, _dot_nt(q2, blk["kc"]), NEG)
                return sp, sc

            def finish_chain(u, s):
                blk = block(u)
                m, p = _softmax_2blk(*s)
                pv = _dot(p, blk["vv"])
                dn = _dot(p, ones)
                q_rows = blk["q_rows"]
                num = jnp.where(low, pv[:BAND], pv[BAND:])
                den = jnp.where(low, dn[:BAND], dn[BAND:])
                m = jnp.where(low, m[:BAND], m[BAND:])
                if pi > 0:
                    m_old = m_ref[q_rows, :]
                    big = jnp.maximum(m_old, m)
                    a_old = jnp.exp(m_old - big)
                    a_new = jnp.exp(m - big)
                    num = a_old * num_ref[q_rows, :] + a_new * num
                    den = a_old * den_ref[q_rows, :] + a_new * den
                    m = big
                num_ref[q_rows, :] = num
                den_ref[q_rows, :] = den
                m_ref[q_rows, :] = m

            _software_pipeline(BLOCKS_PER_ITER, logits, finish_chain, BAND_DEPTH)
            return carry

        assert (r * nblk) % BLOCKS_PER_ITER == 0
        lax.fori_loop(0, r * nblk // BLOCKS_PER_ITER, body, 0)

    def finish(i, carry):
        sl = pl.ds(pl.multiple_of(i * BAND, BAND), BAND)
        o_ref[sl, :] = num_ref[sl, :] / den_ref[sl, :]
        return carry

    lax.fori_loop(0, seq // BAND, finish, 0)


def _prompt_a(qa, ka, va, n_seq, seq):
    n = qa.shape[0]
    blk = pl.BlockSpec((seq, LANES), lambda b, hp: (b, hp))
    return pl.pallas_call(
        _prompt_a_kernel,
        grid=(n_seq, A_W // LANES),
        in_specs=[blk, blk, blk],
        out_specs=blk,
        out_shape=jax.ShapeDtypeStruct((n, A_W), F32),
        scratch_shapes=[pltpu.VMEM((seq, LANES), F32)] * 3,
        compiler_params=_params("parallel", "parallel"),
        name="prompt_a",
    )(qa, ka, va)


def _prompt_b_kernel(sink_ref, q_ref, k_t_ref, v_t_ref, o_ref):
    seq = q_ref.shape[0]
    g = pl.program_id(1)
    ri = lax.broadcasted_iota(jnp.int32, (BAND, BAND), 0)
    ci = lax.broadcasted_iota(jnp.int32, (BAND, BAND), 1)
    prev_ok = ci >= ri
    cur_ok = ci <= ri

    def body(it, carry):
        blocks = {}

        def block(u):
            if u not in blocks:
                n = it * BLOCKS_PER_ITER + u
                cur = pl.ds(pl.multiple_of(n * BAND, BAND), BAND)
                prev = pl.ds(pl.multiple_of(jnp.maximum(n - 1, 0) * BAND, BAND), BAND)
                vv = jnp.concatenate([v_t_ref[:, prev], v_t_ref[:, cur]], axis=1).astype(BF16)
                blocks[u] = dict(
                    cur=cur, q=q_ref[cur, :], kc=k_t_ref[:, cur].astype(BF16), kp=k_t_ref[:, prev].astype(BF16),
                    v_and_ones=jnp.concatenate([vv, jnp.ones_like(vv)], axis=0),
                    no_prev=jnp.where(n > 0, 0.0, NEG), outs=[])
            return blocks[u]

        def logits(k):
            u, hh = divmod(k, G_B)
            blk = block(u)
            qh = blk["q"][:, hh * HEAD_DIM:(hh + 1) * HEAD_DIM].astype(BF16)
            sp = jnp.where(prev_ok, _dot(qh, blk["kp"]), NEG) + blk["no_prev"]
            sc = jnp.where(cur_ok, _dot(qh, blk["kc"]), NEG)
            return sp, sc

        def finish_chain(k, s):
            u, hh = divmod(k, G_B)
            blk = block(u)
            m, p = _softmax_2blk(*s)
            num_den = _dot_nt(p, blk["v_and_ones"])
            num, den = num_den[:, :HEAD_DIM], num_den[:, HEAD_DIM:]
            sink = sink_ref[g * G_B + hh]
            big = jnp.maximum(m, sink)
            a = jnp.exp(m - big)
            blk["outs"].append(num * a / (den * a + jnp.exp(sink - big)))
            if hh + 1 == G_B:
                o_ref[blk["cur"], :] = jnp.concatenate(blk["outs"], axis=1)

        _software_pipeline(BLOCKS_PER_ITER * G_B, logits, finish_chain, BAND_DEPTH)
        return carry

    assert (seq // BAND) % BLOCKS_PER_ITER == 0
    lax.fori_loop(0, seq // BAND // BLOCKS_PER_ITER, body, 0)


def _prompt_b(sink, qb, kb_t, vb_t, n_seq, seq):
    n = qb.shape[0]
    gw = G_B * HEAD_DIM
    q_spec = pl.BlockSpec((seq, gw), lambda b, g: (b, g))
    kv_spec = pl.BlockSpec((None, HEAD_DIM, seq), lambda b, g: (b, g, 0))
    return pl.pallas_call(
        _prompt_b_kernel,
        grid=(n_seq, N_KV_B),
        in_specs=[pl.BlockSpec(memory_space=pltpu.SMEM), q_spec, kv_spec, kv_spec],
        out_specs=q_spec,
        out_shape=jax.ShapeDtypeStruct((n, BQ_W), F32),
        compiler_params=_params("parallel", "parallel"),
        name="prompt_b",
    )(sink, qb, kb_t, vb_t)


def _shifted_buffer(cache, new):
    return jnp.concatenate([cache[:, new.shape[1]:], new], axis=1)


def _sample_a_kernel(q_ref, ck_ref, cv_ref, nk_ref, nv_ref, *rest):
    o_ref, ok_ref, ov_ref = rest[-3:]
    t_new, buf = q_ref.shape[0], ck_ref.shape[1]
    q = q_ref[...].astype(BF16)
    kc, vc = ck_ref[...], cv_ref[...]
    kn, vn = nk_ref[...], nv_ref[...]

    def count(d):
        c = jnp.zeros(d.shape, F32)
        for window, r in A_PATTERNS:
            assert r & (r - 1) == 0
            c = c + jnp.where(((d & (r - 1)) == 0) & (d <= window) & (d >= 0), 1.0, 0.0)
        return c

    d_c = buf + lax.broadcasted_iota(jnp.int32, (t_new, buf), 0) - lax.broadcasted_iota(jnp.int32, (t_new, buf), 1)
    d_n = lax.broadcasted_iota(jnp.int32, (t_new, t_new), 0) - lax.broadcasted_iota(jnp.int32, (t_new, t_new), 1)
    w_c, w_n = count(d_c), count(d_n)
    s_c = jnp.where(w_c > 0, _dot(q, kc.astype(BF16)), NEG)
    s_n = jnp.where(w_n > 0, _dot(q, kn.astype(BF16)), NEG)
    m = jnp.maximum(jnp.max(s_c, axis=1, keepdims=True), jnp.max(s_n, axis=1, keepdims=True))
    e_c = jnp.exp(s_c - m) * w_c
    e_n = jnp.exp(s_n - m) * w_n
    den = jnp.sum(e_c, axis=1, keepdims=True) + jnp.sum(e_n, axis=1, keepdims=True)
    num = _dot_nt(e_c.astype(BF16), vc.astype(BF16)) + _dot_nt(e_n.astype(BF16), vn.astype(BF16))
    o_ref[...] = num / den
    ok_ref[...] = _shifted_buffer(kc, kn)
    ov_ref[...] = _shifted_buffer(vc, vn)


def _sample_a(q, cache_k_t, cache_v_t, new_k_t, new_v_t, e, prev_out):
    n_e, n_b, n_h, _, buf = cache_k_t.shape
    t_new = q.shape[2]
    q_spec = pl.BlockSpec((None, None, t_new, HEAD_DIM), lambda b, h: (b, h, 0, 0))
    c_spec = pl.BlockSpec((None, None, None, HEAD_DIM, buf), lambda b, h: (e, b, h, 0, 0))
    n_spec = pl.BlockSpec((None, None, HEAD_DIM, t_new), lambda b, h: (b, h, 0, 0))
    in_specs = [q_spec, c_spec, c_spec, n_spec, n_spec]
    args = [q, cache_k_t, cache_v_t, new_k_t, new_v_t]
    aliases = {}
    if prev_out is not None:
        in_specs += [pl.BlockSpec(memory_space=pl.ANY)] * 2
        args += list(prev_out)
        aliases = {5: 1, 6: 2}
    return pl.pallas_call(
        _sample_a_kernel,
        grid=(n_b, n_h),
        in_specs=in_specs,
        out_specs=[q_spec, c_spec, c_spec],
        out_shape=[jax.ShapeDtypeStruct(q.shape, F32),
                   jax.ShapeDtypeStruct(cache_k_t.shape, F32),
                   jax.ShapeDtypeStruct(cache_v_t.shape, F32)],
        input_output_aliases=aliases,
        compiler_params=_params("parallel", "parallel"),
        name="sample_a",
    )(*args)


def _sample_b_kernel(sink_ref, q_ref, ck_ref, cv_ref, nk_ref, nv_ref, *rest):
    o_ref, ok_ref, ov_ref = rest[-3:]
    rows, buf, t_new = q_ref.shape[1], ck_ref.shape[2], nk_ref.shape[2]
    t_c = lax.broadcasted_iota(jnp.int32, (rows, buf), 0) & (t_new - 1)
    d_c = buf + t_c - lax.broadcasted_iota(jnp.int32, (rows, buf), 1)
    t_n = lax.broadcasted_iota(jnp.int32, (rows, t_new), 0) & (t_new - 1)
    d_n = t_n - lax.broadcasted_iota(jnp.int32, (rows, t_new), 1)
    ok_c = d_c <= WINDOW_B
    ok_n = (d_n >= 0) & (d_n <= WINDOW_B)
    head = lax.broadcasted_iota(jnp.int32, (rows, 1), 0) // t_new
    for g in range(N_KV_B):
        q = q_ref[g].astype(BF16)
        kc, vc = ck_ref[g], cv_ref[g]
        kn, vn = nk_ref[g], nv_ref[g]
        s_c = jnp.where(ok_c, _dot(q, kc.astype(BF16)), NEG)
        s_n = jnp.where(ok_n, _dot(q, kn.astype(BF16)), NEG)
        m = jnp.maximum(jnp.max(s_c, axis=1, keepdims=True), jnp.max(s_n, axis=1, keepdims=True))
        e_c = jnp.exp(s_c - m)
        e_n = jnp.exp(s_n - m)
        den = jnp.sum(e_c, axis=1, keepdims=True) + jnp.sum(e_n, axis=1, keepdims=True)
        num = _dot_nt(e_c.astype(BF16), vc.astype(BF16)) + _dot_nt(e_n.astype(BF16), vn.astype(BF16))
        sink = jnp.zeros((rows, 1), F32)
        for hh in range(G_B):
            sink = jnp.where(head == hh, sink_ref[g * G_B + hh], sink)
        big = jnp.maximum(m, sink)
        a = jnp.exp(m - big)
        o_ref[g] = num * a / (den * a + jnp.exp(sink - big))
        ok_ref[g] = _shifted_buffer(kc, kn)
        ov_ref[g] = _shifted_buffer(vc, vn)


def _sample_b(sink, q, cache_k_t, cache_v_t, new_k_t, new_v_t, e, prev_out):
    n_e, n_b, n_kv, _, buf = cache_k_t.shape
    rows, t_new = q.shape[2], new_k_t.shape[3]
    assert t_new & (t_new - 1) == 0
    q_spec = pl.BlockSpec((None, n_kv, rows, HEAD_DIM), lambda b: (b, 0, 0, 0))
    c_spec = pl.BlockSpec((None, None, n_kv, HEAD_DIM, buf), lambda b: (e, b, 0, 0, 0))
    n_spec = pl.BlockSpec((None, n_kv, HEAD_DIM, t_new), lambda b: (b, 0, 0, 0))
    in_specs = [pl.BlockSpec(memory_space=pltpu.SMEM), q_spec, c_spec, c_spec, n_spec, n_spec]
    args = [sink, q, cache_k_t, cache_v_t, new_k_t, new_v_t]
    aliases = {}
    if prev_out is not None:
        in_specs += [pl.BlockSpec(memory_space=pl.ANY)] * 2
        args += list(prev_out)
        aliases = {6: 1, 7: 2}
    return pl.pallas_call(
        _sample_b_kernel,
        grid=(n_b,),
        in_specs=in_specs,
        out_specs=[q_spec, c_spec, c_spec],
        out_shape=[jax.ShapeDtypeStruct(q.shape, F32),
                   jax.ShapeDtypeStruct(cache_k_t.shape, F32),
                   jax.ShapeDtypeStruct(cache_v_t.shape, F32)],
        input_output_aliases=aliases,
        compiler_params=_params("parallel"),
        name="sample_b",
    )(*args)


def _upper_ones(n):
    return jnp.where(lax.broadcasted_iota(jnp.int32, (n, n), 0) <= lax.broadcasted_iota(jnp.int32, (n, n), 1),
                     1.0, 0.0).astype(F32)


def _cumsum_lanes(x, with_total=False):
    rows, n = x.shape
    hi = x.astype(BF16)
    rest = x - hi.astype(F32)
    mid = rest.astype(BF16)
    lo = (rest - mid.astype(F32)).astype(BF16)
    rhs = _upper_ones(n).astype(BF16)
    if with_total:
        rhs = jnp.concatenate([rhs, jnp.ones((n, n), BF16)], axis=1)
    parts = _dot(jnp.concatenate([hi, mid, lo], axis=0), rhs)
    both = parts[0:rows] + parts[rows:2 * rows] + parts[2 * rows:3 * rows]
    return (both[:, :n], both[:, n:]) if with_total else both


def _cum_kernel(lf_ref, o_ref, carry_ref):
    @pl.when(pl.program_id(1) == 0)
    def _():
        carry_ref[...] = jnp.zeros_like(carry_ref)

    width = lf_ref.shape[1]
    c = _cumsum_lanes(lf_ref[...]) + carry_ref[:, 0:1]
    o_ref[...] = c * LOG2E
    carry_ref[...] = jnp.broadcast_to(c[:, width - 1:width], carry_ref.shape)


def _prompt_cum(lf_t, n_seq, seq):
    heads = lf_t.shape[1]
    spec = pl.BlockSpec((None, heads, CUM_BLOCK), lambda b, j: (b, 0, j))
    return pl.pallas_call(
        _cum_kernel,
        grid=(n_seq, seq // CUM_BLOCK),
        in_specs=[spec],
        out_specs=spec,
        out_shape=jax.ShapeDtypeStruct((n_seq, heads, seq), F32),
        scratch_shapes=[pltpu.VMEM((heads, LANES), F32)],
        compiler_params=_params("parallel", "arbitrary"),
        name="prompt_cum",
    )(lf_t)


def _fox_prompt_kernel(q_ref, k_t_ref, v_t_ref, cum_ref, o_ref, m_ref, acc_ref):
    i, j = pl.program_id(1), pl.program_id(2)
    tq, tk = q_ref.shape[0], k_t_ref.shape[1]

    @pl.when(j == 0)
    def _():
        m_ref[...] = jnp.full(m_ref.shape, NEG, F32)
        acc_ref[...] = jnp.zeros_like(acc_ref)

    def step(masked):
        def body(r, carry):
            rows = pl.ds(pl.multiple_of(r * FOX_SUB, FOX_SUB), FOX_SUB)
            if masked:
                q_pos = i * tq + r * FOX_SUB + lax.broadcasted_iota(jnp.int32, (FOX_SUB, tk), 0)
                causal = j * tk + lax.broadcasted_iota(jnp.int32, (FOX_SUB, tk), 1) <= q_pos
            kv = {}

            def group(g):
                if g not in kv:
                    v_t = v_t_ref[g * HEAD_DIM:(g + 1) * HEAD_DIM, :].astype(BF16)
                    kv[g] = (k_t_ref[g * HEAD_DIM:(g + 1) * HEAD_DIM, :].astype(BF16),
                             jnp.concatenate([v_t, jnp.ones_like(v_t)], axis=0))
                return kv[g]

            def logits(h):
                qh = q_ref[rows, h * HEAD_DIM:(h + 1) * HEAD_DIM].astype(BF16)
                return _dot(qh, group(h // G_C)[0])

            def finish_chain(h, s):
                s = s - cum_ref[h:h + 1, :]
                if masked:
                    s = jnp.where(causal, s, NEG)
                m_old = m_ref[h, rows, :]
                m_new = jnp.maximum(m_old, jnp.max(s, axis=1, keepdims=True))
                p = jnp.exp2(s - m_new).astype(BF16)
                acc_ref[h, rows, :] = jnp.exp2(m_old - m_new) * acc_ref[h, rows, :] + _dot_nt(p, group(h // G_C)[1])
                m_ref[h, rows, :] = m_new

            _software_pipeline(N_HEADS_C, logits, finish_chain, FOX_DEPTH)
            return carry

        lax.fori_loop(0, tq // FOX_SUB, body, 0)

    all_visible = (j + 1) * tk - 1 <= i * tq
    some_visible = j * tk <= i * tq + tq - 1
    pl.when(all_visible)(lambda: step(False))
    pl.when(jnp.logical_and(some_visible, jnp.logical_not(all_visible)))(lambda: step(True))

    @pl.when(j == pl.num_programs(2) - 1)
    def _():
        o_ref[...] = jnp.concatenate(
            [acc_ref[h][:, :HEAD_DIM] / acc_ref[h][:, HEAD_DIM:] for h in range(N_HEADS_C)], axis=1)


def _fox_prompt(q, k_t, v_t, cum_t, n_seq, seq):
    n = q.shape[0]
    nq, nk = seq // FOX_TQ, seq // FOX_TK

    def kv_idx(b, i, j):
        return (b, 0, jnp.minimum(j, (i * FOX_TQ + FOX_TQ - 1) // FOX_TK))

    q_spec = pl.BlockSpec((FOX_TQ, CQ_W), lambda b, i, j: (b * nq + i, 0))
    return pl.pallas_call(
        _fox_prompt_kernel,
        grid=(n_seq, nq, nk),
        in_specs=[q_spec,
                  pl.BlockSpec((None, CKV_W, FOX_TK), kv_idx),
                  pl.BlockSpec((None, CKV_W, FOX_TK), kv_idx),
                  pl.BlockSpec((None, N_HEADS_C, FOX_TK), kv_idx)],
        out_specs=q_spec,
        out_shape=jax.ShapeDtypeStruct((n, CQ_W), F32),
        scratch_shapes=[pltpu.VMEM((N_HEADS_C, FOX_TQ, 1), F32),
                        pltpu.VMEM((N_HEADS_C, FOX_TQ, 2 * HEAD_DIM), F32)],
        compiler_params=_params("parallel", "parallel", "arbitrary"),
        name="fox_prompt",
    )(q, k_t, v_t, cum_t)


def _fox_sample_kernel(pt_ref, q_ref, *refs):
    npg = PAGES_PER_STEP
    k_pages, v_pages, lf_pages = refs[0:npg], refs[npg:2 * npg], refs[2 * npg:3 * npg]
    nk_ref, nv_ref, nlf_ref, o_ref, m_ref, l_ref, acc_ref, pref_ref = refs[3 * npg:]
    j = pl.program_id(1)
    page = lf_pages[0].shape[1]
    t_new = nk_ref.shape[2]
    rows = N_HEADS_C * t_new

    @pl.when(j == 0)
    def _():
        m_ref[...] = jnp.full(m_ref.shape, NEG, F32)
        l_ref[...] = jnp.zeros_like(l_ref)
        acc_ref[...] = jnp.zeros_like(acc_ref)
        pref_ref[...] = jnp.zeros_like(pref_ref)

    def per_row(c):
        return jnp.concatenate([c] * t_new, axis=0)

    def stacked(ref):
        return ref[...].reshape(CKV_W, ref.shape[2]).astype(BF16)

    def update(s_parts, v_parts):
        m_old = m_ref[...]
        m_new = m_old
        for s in s_parts:
            m_new = jnp.maximum(m_new, jnp.max(s, axis=1, keepdims=True))
        alpha = jnp.exp2(m_old - m_new)
        l_new = alpha * l_ref[...]
        acc = alpha * acc_ref[...]
        for s, v_t in zip(s_parts, v_parts):
            p = jnp.exp2(s - m_new)
            l_new = l_new + jnp.sum(p, axis=1, keepdims=True)
            acc = acc + _dot_nt(p.astype(BF16), v_t)
        l_ref[...] = l_new
        acc_ref[...] = acc
        m_ref[...] = m_new

    def chunk(refs, c):
        return jnp.concatenate([stacked(r) for r in refs[c * FOX_PAGE_CHUNK:(c + 1) * FOX_PAGE_CHUNK]], axis=1)

    q = q_ref[...]
    n_chunks = npg // FOX_PAGE_CHUNK
    qk = [_dot(q, chunk(k_pages, c)) for c in range(n_chunks)]
    local, total = _cumsum_lanes(jnp.concatenate([r[...] for r in lf_pages], axis=0), with_total=True)
    pref = pref_ref[...]
    bias = []
    for i in range(npg):
        sl = slice(i * N_HEADS_C, (i + 1) * N_HEADS_C)
        bias.append(per_row((local[sl] + pref) * LOG2E))
        pref = pref + total[sl]
    pref_ref[...] = pref
    update([qk[c] - jnp.concatenate(bias[c * FOX_PAGE_CHUNK:(c + 1) * FOX_PAGE_CHUNK], axis=1)
            for c in range(n_chunks)], [chunk(v_pages, c) for c in range(n_chunks)])

    @pl.when(j == pl.num_programs(1) - 1)
    def _():
        lf = nlf_ref[...]
        lane = lax.broadcasted_iota(jnp.int32, lf.shape, 1)
        cum_n = pref[:, :t_new]
        for t in range(t_new):
            cum_n = cum_n + jnp.where(lane >= t, lf[:, t:t + 1], 0.0)
        tok = lax.broadcasted_iota(jnp.int32, (rows, t_new), 0) // N_HEADS_C
        causal = lax.broadcasted_iota(jnp.int32, (rows, t_new), 1) <= tok
        s = _dot(q, stacked(nk_ref)) - per_row(cum_n * LOG2E)
        update([jnp.where(causal, s, NEG)], [stacked(nv_ref)])
        out = acc_ref[...] / l_ref[...]
        kv_head = (lax.broadcasted_iota(jnp.int32, (rows, HEAD_DIM), 0) % N_HEADS_C) // G_C
        o = out[:, 0:HEAD_DIM]
        for g in range(1, N_KV_C):
            o = jnp.where(kv_head == g, out[:, g * HEAD_DIM:(g + 1) * HEAD_DIM], o)
        o_ref[...] = o


def _fox_sample(page_table, q, cache_k_t, cache_v_t, cache_lf_t, new_k_t, new_v_t, new_lf_t, o):
    n_b, rows, _ = q.shape
    page = cache_k_t.shape[-1]
    t_new = new_k_t.shape[-1]
    n_pages = page_table.shape[1]
    npg = PAGES_PER_STEP
    assert t_new & (t_new - 1) == 0 and n_pages % npg == 0

    def page_spec(shape, i):
        return pl.BlockSpec((None, None) + shape, lambda b, j, pt: (o, pt[b, j * npg + i]) + (0,) * len(shape))

    q_spec = pl.BlockSpec((None, rows, CKV_W), lambda b, j, pt: (b, 0, 0))
    o_spec = pl.BlockSpec((None, rows, HEAD_DIM), lambda b, j, pt: (b, 0, 0))
    in_specs = [q_spec]
    in_specs += [page_spec((N_KV_C, HEAD_DIM, page), i) for i in range(npg)]
    in_specs += [page_spec((N_KV_C, HEAD_DIM, page), i) for i in range(npg)]
    in_specs += [page_spec((N_HEADS_C, page), i) for i in range(npg)]
    in_specs += [pl.BlockSpec((None, N_KV_C, HEAD_DIM, t_new), lambda b, j, pt: (b, 0, 0, 0))] * 2
    in_specs += [pl.BlockSpec((None, N_HEADS_C, t_new), lambda b, j, pt: (b, 0, 0))]
    grid_spec = pltpu.PrefetchScalarGridSpec(
        num_scalar_prefetch=1,
        grid=(n_b, n_pages // npg),
        in_specs=in_specs,
        out_specs=o_spec,
        scratch_shapes=[pltpu.VMEM((rows, 1), F32), pltpu.VMEM((rows, 1), F32),
                        pltpu.VMEM((rows, CKV_W), F32), pltpu.VMEM((N_HEADS_C, LANES), F32)],
    )
    return pl.pallas_call(
        _fox_sample_kernel,
        grid_spec=grid_spec,
        out_shape=jax.ShapeDtypeStruct((n_b, rows, HEAD_DIM), F32),
        compiler_params=_params("parallel", "arbitrary"),
        name="fox_sample",
    )(page_table, q, *([cache_k_t] * npg), *([cache_v_t] * npg), *([cache_lf_t] * npg),
      new_k_t, new_v_t, new_lf_t)


def _rope_tables(seq, n_sample_seq, t_new, past_len):
    inv_freq = ROPE_THETA ** (-jnp.arange(HALF, dtype=F32) / HALF)

    def tables(pos):
        ang = pos.astype(F32)[:, None] * inv_freq[None, :]
        return jnp.cos(ang), jnp.sin(ang)

    cos_p, sin_p = tables(jnp.arange(seq))
    cos_s, sin_s = tables(past_len + jnp.arange(t_new))
    return cos_p, sin_p, cos_s, sin_s


def kernel(x_prompt, x_sample, cache_a_k, cache_a_v, cache_b_k, cache_b_v, cache_c_k, cache_c_v, cache_c_logf, page_table, p_prompt, p_sample, g_ffn1, w_ffn1_in, w_ffn1_out, g_mix, w_in_ab, w_out_ab, sink_b, w_in_c, b_forget, w_out_c, g_ffn2, w_ffn2_in, w_ffn2_out, g_ple, w_ple_gate, w_ple_proj, g_final):
    n_seq, seq, d = x_prompt.shape
    n_dec, t_new, _ = x_sample.shape
    depth = g_ffn1.shape[0]
    n_p, n_s = n_seq * seq, n_dec * t_new
    n = n_p + n_s
    past_len = page_table.shape[1] * cache_c_k.shape[2]
    assert n_p % ROW_TILE == 0 and n_s % ROW_TILE == 0 and seq % ROW_TILE == 0

    x_in = [x_prompt.reshape(n_p, d), x_sample.reshape(n_s, d)]
    p_rows = (p_prompt.reshape(depth, n_p, -1), p_sample.reshape(depth, n_s, -1))

    cos_p, sin_p, cos_s, sin_s = _rope_tables(seq, n_dec, t_new, past_len)
    cos = jnp.concatenate([jnp.tile(cos_p, (n_seq, 1)), jnp.tile(cos_s, (n_dec, 1))], axis=0)
    sin = jnp.concatenate([jnp.tile(sin_p, (n_seq, 1)), jnp.tile(sin_s, (n_dec, 1))], axis=0)
    reps = LANES // HEAD_DIM
    tables = (jnp.tile(cos, (1, 2 * reps)), jnp.tile(jnp.concatenate([-sin, sin], axis=1), (1, reps)), cos.T, sin.T)

    bf = lambda w: w.astype(BF16)
    g3 = lambda g: g.reshape(depth, 1, d)
    g_ffn1, g_mix, g_ffn2, g_ple = g3(g_ffn1), g3(g_mix), g3(g_ffn2), g3(g_ple)
    w_ffn1_in, w_ffn1_out, w_ffn2_in, w_ffn2_out = bf(w_ffn1_in), bf(w_ffn1_out), bf(w_ffn2_in), bf(w_ffn2_out)
    w_ple_gate, w_ple_proj, w_out_ab, w_out_c = bf(w_ple_gate), bf(w_ple_proj), bf(w_out_ab), bf(w_out_c)
    qb_lo, kb_lo = 3 * A_W, 3 * A_W + BQ_W
    w_ab_rows = bf(w_in_ab[:, :, :kb_lo])
    w_ab_t = bf(jnp.swapaxes(jnp.concatenate([w_in_ab[:, :, A_W:3 * A_W], w_in_ab[:, :, kb_lo:]], axis=2), 1, 2))
    w_c_q = bf(w_in_c[:, :, :CQ_W])
    w_c_t = bf(jnp.swapaxes(w_in_c[:, :, CQ_W:], 1, 2))
    b_forget3 = b_forget.reshape(b_forget.shape[0], N_HEADS_C, 1)

    tr = lambda c: jnp.transpose(c, (0, 1, 3, 4, 2))
    cache_a_k_t, cache_a_v_t, cache_b_k_t, cache_b_v_t = tr(cache_a_k), tr(cache_a_v), tr(cache_b_k), tr(cache_b_v)
    cache_c_k_t, cache_c_v_t = tr(cache_c_k), tr(cache_c_v)
    cache_c_lf_t = jnp.transpose(cache_c_logf, (0, 1, 3, 2))

    assert n_s <= seq

    def sample_cols(a_t, heads):
        return a_t[n_seq, :, :n_s].reshape(heads, HEAD_DIM, n_dec, t_new).transpose(2, 0, 1, 3)

    def prompt_cols(a_t, heads, keep):
        a = a_t[:n_seq, :, seq - keep:].reshape(n_seq, heads, HEAD_DIM, keep)
        return a.transpose(0, 3, 1, 2)

    kv_of_head = jnp.repeat(jnp.eye(N_KV_C, dtype=F32), G_C, axis=0)

    buf_a, buf_b = cache_a_k.shape[2], cache_b_k.shape[2]
    keep_a, keep_b = min(buf_a, seq), min(buf_b, seq)
    ak_p, av_p, bk_p, bv_p = [], [], [], []
    ck_p, cv_p, cl_p, ck_s, cv_s, cl_s = [], [], [], [], [], []
    a_out, b_out = None, None

    for l in range(depth):
        x = _ffn(x_in if l == 0 else [x], g_ffn1, w_ffn1_in, w_ffn1_out, l)
        if l % 2 == 0:
            e = l // 2
            qa, ka, va, qb, ka_t, va_t, kb_t, vb_t = _proj_ab(x, g_mix, w_ab_rows, w_ab_t, tables, l, e, seq)
            oa = _prompt_a(qa, ka, va, n_seq, seq)
            ob = _prompt_b(sink_b[e], qb, kb_t, vb_t, n_seq, seq)
            qa_s = qa[n_p:].reshape(n_dec, t_new, N_HEADS_A, HEAD_DIM).transpose(0, 2, 1, 3)
            a_res = _sample_a(qa_s, cache_a_k_t, cache_a_v_t, sample_cols(ka_t, N_HEADS_A),
                              sample_cols(va_t, N_HEADS_A), e, a_out)
            oa_s, a_out = a_res[0], a_res[1:]
            qb_s = qb[n_p:].reshape(n_dec, t_new, N_KV_B, G_B, HEAD_DIM).transpose(0, 2, 3, 1, 4)
            qb_s = qb_s.reshape(n_dec, N_KV_B, G_B * t_new, HEAD_DIM)
            b_res = _sample_b(sink_b[e], qb_s, cache_b_k_t, cache_b_v_t, sample_cols(kb_t, N_KV_B),
                              sample_cols(vb_t, N_KV_B), e, b_out)
            ob_s, b_out = b_res[0], b_res[1:]
            oa_s = oa_s.transpose(0, 2, 1, 3).reshape(n_s, A_W)
            ob_s = ob_s.reshape(n_dec, N_KV_B, G_B, t_new, HEAD_DIM).transpose(0, 3, 1, 2, 4).reshape(n_s, BQ_W)
            oa = lax.dynamic_update_slice(oa, oa_s, (n_p, 0))
            ob = lax.dynamic_update_slice(ob, ob_s, (n_p, 0))
            x = _out_proj(x, [oa, ob], w_out_ab, e)
            ak_p.append(prompt_cols(ka_t, N_HEADS_A, keep_a))
            av_p.append(prompt_cols(va_t, N_HEADS_A, keep_a))
            bk_p.append(prompt_cols(kb_t, N_KV_B, keep_b))
            bv_p.append(prompt_cols(vb_t, N_KV_B, keep_b))
        else:
            o = l // 2
            qc, kc_t, vc_t, lf_t = _proj_c(x, g_mix, w_c_q, w_c_t, b_forget3, l, o, seq)
            cum_t = _prompt_cum(lf_t, n_seq, seq)
            oc = _fox_prompt(qc, kc_t, vc_t, cum_t, n_seq, seq)
            qc_s = qc[n_p:].reshape(n_dec, t_new, N_HEADS_C, 1, HEAD_DIM)
            qc_s = (qc_s * kv_of_head[None, None, :, :, None]).astype(BF16)
            qc_s = qc_s.reshape(n_dec, t_new * N_HEADS_C, CKV_W)
            lf_s = lf_t[n_seq, :, :n_s].reshape(N_HEADS_C, n_dec, t_new)
            new_lf_t = lf_s.transpose(1, 0, 2)
            oc_s = _fox_sample(page_table, qc_s, cache_c_k_t, cache_c_v_t, cache_c_lf_t,
                               sample_cols(kc_t, N_KV_C), sample_cols(vc_t, N_KV_C), new_lf_t, o)
            oc = lax.dynamic_update_slice(oc, oc_s.reshape(n_s, CQ_W), (n_p, 0))
            x = _out_proj(x, [oc], w_out_c, o)
            ck_p.append(prompt_cols(kc_t, N_KV_C, seq))
            cv_p.append(prompt_cols(vc_t, N_KV_C, seq))
            cl_p.append(lf_t[:n_seq].transpose(0, 2, 1))
            ck_s.append(sample_cols(kc_t, N_KV_C).transpose(0, 3, 1, 2))
            cv_s.append(sample_cols(vc_t, N_KV_C).transpose(0, 3, 1, 2))
            cl_s.append(lf_s.transpose(1, 2, 0))
        x = _ffn([x], g_ffn2, w_ffn2_in, w_ffn2_out, l)
        x = _ple(x, *p_rows, g_ple, w_ple_gate, w_ple_proj, l)

    y = _final_norm(x, g_final.reshape(1, d))
    back = lambda c: jnp.transpose(c, (0, 1, 4, 2, 3))
    return (y[:n_p].reshape(n_seq, seq, d), y[n_p:].reshape(n_dec, t_new, d),
            jnp.stack(ak_p), jnp.stack(av_p), back(a_out[0]), back(a_out[1]),
            jnp.stack(bk_p), jnp.stack(bv_p), back(b_out[0]), back(b_out[1]),
            jnp.stack(ck_p), jnp.stack(cv_p), jnp.stack(cl_p),
            jnp.stack(ck_s), jnp.stack(cv_s), jnp.stack(cl_s))
```

```python
import functools

import jax
import jax.numpy as jnp
from jax import lax
from jax.experimental import pallas as pl
from jax.experimental.pallas import tpu as pltpu

F32 = jnp.float32
BF16 = jnp.bfloat16

HEAD_DIM = 64
HALF = HEAD_DIM // 2
ATTN_SCALE = HEAD_DIM ** -0.5
A_PATTERNS = ((128, 1), (512, 4), (2048, 16))
N_HEADS_A = 8
N_HEADS_B = 8
N_KV_B = 2
G_B = N_HEADS_B // N_KV_B
WINDOW_B = 128
N_HEADS_C = 16
N_KV_C = 4
G_C = N_HEADS_C // N_KV_C
ROPE_THETA = 10000.0
RMS_EPS = 1e-6
NEG = -1e30

A_W = N_HEADS_A * HEAD_DIM
BQ_W = N_HEADS_B * HEAD_DIM
BKV_W = N_KV_B * HEAD_DIM
CQ_W = N_HEADS_C * HEAD_DIM
CKV_W = N_KV_C * HEAD_DIM

VMEM_LIMIT = 56 * 1024 * 1024
LANES = 128
ROW_TILE = 512
FF_CHUNK = 256
BAND = 128
FOX_TQ = 512
FOX_TK = 512
FOX_SUB = 256
FOX_DEPTH = 15
CUM_BLOCK = 512
PAGES_PER_STEP = 32
BLOCKS_PER_ITER = 4
BAND_DEPTH = 2
FOX_PAGE_CHUNK = 4
LOG2E = 1.4426950408889634


def _dot(a, b):
    return jnp.dot(a, b, preferred_element_type=F32)


def _dot_nt(a, b):
    return lax.dot_general(a, b, (((1,), (1,)), ((), ())), preferred_element_type=F32)


def _dot_exact(a, b):
    return jnp.dot(a, b, preferred_element_type=F32, precision=lax.Precision.HIGHEST)


def _rms(x, g):
    return x * lax.rsqrt(jnp.mean(x * x, axis=-1, keepdims=True) + RMS_EPS) * g


def _params(*sem):
    return pltpu.CompilerParams(dimension_semantics=sem, vmem_limit_bytes=VMEM_LIMIT)


def _resident(shape, index_map):
    return pl.BlockSpec(shape, index_map, pipeline_mode=pl.Buffered(1))


def _split_row_specs(n_first, width, lead=()):
    tiles = n_first // ROW_TILE
    pad = (None,) * len(lead)
    return [pl.BlockSpec(pad + (ROW_TILE, width), lambda i: lead + (jnp.minimum(i, tiles - 1), 0)),
            pl.BlockSpec(pad + (ROW_TILE, width), lambda i: lead + (jnp.maximum(i - tiles, 0), 0))]


def _ffn_kernel(*refs, first_tiles):
    x_refs, (g_ref, win_ref, wout_ref, o_ref, act_ref) = refs[:-5], refs[-5:]
    d_ff = wout_ref.shape[0]
    x = x_refs[0][...]
    if len(x_refs) == 2:
        x = jnp.where(pl.program_id(0) < first_tiles, x, x_refs[1][...])
    h = _rms(x, g_ref[...]).astype(BF16)
    for c in range(d_ff // FF_CHUNK):
        lo = c * FF_CHUNK
        a = _dot(h, win_ref[:, lo:lo + FF_CHUNK])
        b = _dot(h, win_ref[:, d_ff + lo:d_ff + lo + FF_CHUNK])
        act_ref[:, lo:lo + FF_CHUNK] = (a * jax.nn.sigmoid(a) * b).astype(BF16)
    o_ref[...] = x + 0.5 * _dot(act_ref[...], wout_ref[...])


def _ffn(xs, g, w_in, w_out, layer):
    n, d = sum(x.shape[0] for x in xs), xs[0].shape[1]
    d_ff = w_out.shape[1]
    x_specs = [pl.BlockSpec((ROW_TILE, d), lambda i: (i, 0))] if len(xs) == 1 else _split_row_specs(xs[0].shape[0], d)
    return pl.pallas_call(
        functools.partial(_ffn_kernel, first_tiles=xs[0].shape[0] // ROW_TILE),
        grid=(n // ROW_TILE,),
        in_specs=x_specs + [
            pl.BlockSpec((None, 1, d), lambda i: (layer, 0, 0)),
            _resident((None, d, 2 * d_ff), lambda i: (layer, 0, 0)),
            _resident((None, d_ff, d), lambda i: (layer, 0, 0)),
        ],
        out_specs=pl.BlockSpec((ROW_TILE, d), lambda i: (i, 0)),
        out_shape=jax.ShapeDtypeStruct((n, d), F32),
        scratch_shapes=[pltpu.VMEM((ROW_TILE, d_ff), BF16)],
        compiler_params=_params("parallel"),
        name="ffn",
    )(*xs, g, w_in, w_out)


def _ple_kernel(x_ref, p_first_ref, p_second_ref, g_ref, wg_ref, wp_ref, o_ref, *, first_tiles):
    x = x_ref[...]
    h = _rms(x, g_ref[...]).astype(BF16)
    gate = jax.nn.sigmoid(_dot(h, wg_ref[...]))
    p = jnp.where(pl.program_id(0) < first_tiles, p_first_ref[...], p_second_ref[...])
    o_ref[...] = x + gate * _dot(p.astype(BF16), wp_ref[...])


def _ple(x, p_first, p_second, g, w_gate, w_proj, layer):
    n, d = x.shape
    dp = p_first.shape[-1]
    return pl.pallas_call(
        functools.partial(_ple_kernel, first_tiles=p_first.shape[1] // ROW_TILE),
        grid=(n // ROW_TILE,),
        in_specs=[pl.BlockSpec((ROW_TILE, d), lambda i: (i, 0))]
        + _split_row_specs(p_first.shape[1], dp, lead=(layer,)) + [
            pl.BlockSpec((None, 1, d), lambda i: (layer, 0, 0)),
            _resident((None, d, d), lambda i: (layer, 0, 0)),
            _resident((None, dp, d), lambda i: (layer, 0, 0)),
        ],
        out_specs=pl.BlockSpec((ROW_TILE, d), lambda i: (i, 0)),
        out_shape=jax.ShapeDtypeStruct((n, d), F32),
        compiler_params=_params("parallel"),
        name="ple",
    )(x, p_first, p_second, g, w_gate, w_proj)


def _final_norm_kernel(x_ref, g_ref, o_ref):
    o_ref[...] = _rms(x_ref[...], g_ref[...])


def _final_norm(x, g):
    n, d = x.shape
    return pl.pallas_call(
        _final_norm_kernel,
        grid=(n // ROW_TILE,),
        in_specs=[pl.BlockSpec((ROW_TILE, d), lambda i: (i, 0)), pl.BlockSpec((1, d), lambda i: (0, 0))],
        out_specs=pl.BlockSpec((ROW_TILE, d), lambda i: (i, 0)),
        out_shape=jax.ShapeDtypeStruct((n, d), F32),
        compiler_params=_params("parallel"),
        name="final_norm",
    )(x, g)


def _out_proj_kernel(*refs):
    x_ref, part_refs, w_ref, o_ref = refs[0], refs[1:-2], refs[-2], refs[-1]
    acc = x_ref[...]
    lo = 0
    for p_ref in part_refs:
        width = p_ref.shape[1]
        acc = acc + _dot(p_ref[...].astype(BF16), w_ref[lo:lo + width, :])
        lo += width
    o_ref[...] = acc


def _out_proj(x, parts, w, layer):
    n, d = x.shape
    return pl.pallas_call(
        _out_proj_kernel,
        grid=(n // ROW_TILE,),
        in_specs=[pl.BlockSpec((ROW_TILE, d), lambda i: (i, 0))]
        + [pl.BlockSpec((ROW_TILE, p.shape[1]), lambda i: (i, 0)) for p in parts]
        + [_resident((None, w.shape[1], d), lambda i: (layer, 0, 0))],
        out_specs=pl.BlockSpec((ROW_TILE, d), lambda i: (i, 0)),
        out_shape=jax.ShapeDtypeStruct((n, d), F32),
        compiler_params=_params("parallel"),
        name="out_proj",
    )(x, *parts, w)


def _rope_rows(y, cos, sin_signed, first_half):
    width = y.shape[1]
    partner = jnp.where(first_half, pltpu.roll(y, width - HALF, axis=1), pltpu.roll(y, HALF, axis=1))
    return y * cos + partner * sin_signed


def _rope_store_t(z, cos_t, sin_t, out_ref, n_heads):
    for h in range(n_heads):
        x1 = z[h * HEAD_DIM:h * HEAD_DIM + HALF]
        x2 = z[h * HEAD_DIM + HALF:(h + 1) * HEAD_DIM]
        out_ref[h * HEAD_DIM:h * HEAD_DIM + HALF, :] = x1 * cos_t - x2 * sin_t
        out_ref[h * HEAD_DIM + HALF:(h + 1) * HEAD_DIM, :] = x2 * cos_t + x1 * sin_t


def _proj_ab_kernel(x_ref, g_ref, wr_ref, wt_ref, cos_ref, sin_ref, cos_t_ref, sin_t_ref,
                    qa_ref, ka_ref, va_ref, qb_ref, ka_t_ref, va_t_ref, kb_t_ref, vb_t_ref):
    h = _rms(x_ref[...], g_ref[...]).astype(BF16)
    rows = h.shape[0]
    reps = A_W // LANES
    cos = jnp.concatenate([cos_ref[...]] * reps, axis=1)
    sin = jnp.concatenate([sin_ref[...]] * reps, axis=1)
    lane = lax.broadcasted_iota(jnp.int32, (rows, A_W), 1)
    first_half = (lane & (HEAD_DIM - 1)) < HALF
    rope = functools.partial(_rope_rows, cos=cos, sin_signed=sin, first_half=first_half)
    qa_ref[...] = rope(_dot(h, wr_ref[:, 0:A_W])) * ATTN_SCALE
    ka_ref[...] = rope(_dot(h, wr_ref[:, A_W:2 * A_W]))
    va_ref[...] = _dot(h, wr_ref[:, 2 * A_W:3 * A_W])
    qb_ref[...] = rope(_dot(h, wr_ref[:, 3 * A_W:3 * A_W + BQ_W])) * ATTN_SCALE
    cos_t = cos_t_ref[...]
    sin_t = sin_t_ref[...]
    _rope_store_t(_dot_nt(wt_ref[0:A_W, :], h), cos_t, sin_t, ka_t_ref, N_HEADS_A)
    va_t_ref[...] = _dot_nt(wt_ref[A_W:2 * A_W, :], h)
    _rope_store_t(_dot_nt(wt_ref[2 * A_W:2 * A_W + BKV_W, :], h), cos_t, sin_t, kb_t_ref, N_KV_B)
    vb_t_ref[...] = _dot_nt(wt_ref[2 * A_W + BKV_W:2 * A_W + 2 * BKV_W, :], h)


def _feature_major_out(seq):
    per = seq // ROW_TILE
    return lambda w: pl.BlockSpec((None, w, ROW_TILE), lambda i: (i // per, 0, i % per))


def _proj_ab(x, g, w_rows, w_t, tables, layer, e, seq):
    n, d = x.shape
    slots = -(-n // seq)
    cos, sin, cos_t, sin_t = tables
    row = lambda w: pl.BlockSpec((ROW_TILE, w), lambda i: (i, 0))
    col = lambda w: pl.BlockSpec((w, ROW_TILE), lambda i: (0, i))
    col_out = _feature_major_out(seq)
    sds = jax.ShapeDtypeStruct
    return pl.pallas_call(
        _proj_ab_kernel,
        grid=(n // ROW_TILE,),
        in_specs=[
            row(d),
            pl.BlockSpec((None, 1, d), lambda i: (layer, 0, 0)),
            _resident((None, d, w_rows.shape[2]), lambda i: (e, 0, 0)),
            _resident((None, w_t.shape[1], d), lambda i: (e, 0, 0)),
            row(LANES), row(LANES), col(HALF), col(HALF),
        ],
        out_specs=[row(A_W), row(A_W), row(A_W), row(BQ_W),
                   col_out(A_W), col_out(A_W), col_out(BKV_W), col_out(BKV_W)],
        out_shape=[sds((n, A_W), F32)] * 3 + [sds((n, BQ_W), F32)]
        + [sds((slots, A_W, seq), F32)] * 2 + [sds((slots, BKV_W, seq), F32)] * 2,
        compiler_params=_params("parallel"),
        name="proj_ab",
    )(x, g, w_rows, w_t, cos, sin, cos_t, sin_t)


def _proj_c_kernel(x_ref, g_ref, wq_ref, wt_ref, bf_ref, q_ref, k_t_ref, v_t_ref, lf_t_ref):
    h = _rms(x_ref[...], g_ref[...]).astype(BF16)
    q_ref[...] = _dot(h, wq_ref[...]) * (ATTN_SCALE * LOG2E)
    k_t_ref[...] = _dot_nt(wt_ref[0:CKV_W, :], h)
    v_t_ref[...] = _dot_nt(wt_ref[CKV_W:2 * CKV_W, :], h)
    f_t = _dot_nt(wt_ref[2 * CKV_W:2 * CKV_W + N_HEADS_C, :], h)
    lf_t_ref[...] = jax.nn.log_sigmoid(f_t + bf_ref[...])


def _proj_c(x, g, w_q, w_t, b_forget, layer, o, seq):
    n, d = x.shape
    slots = -(-n // seq)
    row = lambda w: pl.BlockSpec((ROW_TILE, w), lambda i: (i, 0))
    col = _feature_major_out(seq)
    sds = jax.ShapeDtypeStruct
    return pl.pallas_call(
        _proj_c_kernel,
        grid=(n // ROW_TILE,),
        in_specs=[
            row(d),
            pl.BlockSpec((None, 1, d), lambda i: (layer, 0, 0)),
            _resident((None, d, CQ_W), lambda i: (o, 0, 0)),
            _resident((None, w_t.shape[1], d), lambda i: (o, 0, 0)),
            pl.BlockSpec((None, N_HEADS_C, 1), lambda i: (o, 0, 0)),
        ],
        out_specs=[row(CQ_W), col(CKV_W), col(CKV_W), col(N_HEADS_C)],
        out_shape=[sds((n, CQ_W), F32), sds((slots, CKV_W, seq), F32), sds((slots, CKV_W, seq), F32),
                   sds((slots, N_HEADS_C, seq), F32)],
        compiler_params=_params("parallel"),
        name="proj_c",
    )(x, g, w_q, w_t, b_forget)


def _software_pipeline(n, issue, consume, depth=1):
    pending = [issue(k) for k in range(min(depth, n))]
    for k in range(n):
        if k + depth < n:
            pending.append(issue(k + depth))
        consume(k, pending.pop(0))


def _softmax_2blk(sp, sc):
    m = jnp.max(jnp.maximum(sp, sc), axis=1, keepdims=True)
    return m, jnp.concatenate([jnp.exp(sp - m), jnp.exp(sc - m)], axis=1).astype(BF16)


def _prompt_a_kernel(q_ref, k_ref, v_ref, o_ref, num_ref, den_ref, m_ref):
    seq = q_ref.shape[0]
    assert BAND & (BAND - 1) == 0
    ri = lax.broadcasted_iota(jnp.int32, (2 * BAND, BAND), 0) & (BAND - 1)
    ci = lax.broadcasted_iota(jnp.int32, (2 * BAND, BAND), 1)
    prev_ok2 = ci >= ri
    cur_ok2 = ci <= ri
    low = lax.broadcasted_iota(jnp.int32, (BAND, LANES), 1) < HEAD_DIM
    ones = jnp.ones((2 * BAND, LANES), BF16)

    for pi, (window, r) in enumerate(A_PATTERNS):
        assert window // r == BAND
        nblk = seq // (r * BAND)

        def rows(start, r=r):
            return pl.ds(start, BAND) if r == 1 else pl.ds(start, BAND, stride=r)

        def body(it, carry, r=r, nblk=nblk, pi=pi, rows=rows):
            heads = LANES // HEAD_DIM
            blocks = {}

            def block(u):
                if u not in blocks:
                    idx = it * BLOCKS_PER_ITER + u
                    rho = idx // nblk
                    n = idx - rho * nblk
                    q_rows = rows(rho + r * BAND * n)
                    p_rows = rows(rho + r * BAND * jnp.maximum(n - 1, 0))
                    blocks[u] = dict(
                        q_rows=q_rows, q=q_ref[q_rows, :],
                        kc=k_ref[q_rows, :].astype(BF16), kp=k_ref[p_rows, :].astype(BF16),
                        vv=jnp.concatenate([v_ref[p_rows, :], v_ref[q_rows, :]], axis=0).astype(BF16),
                        no_prev=jnp.where(n > 0, 0.0, NEG), nums=[], dens=[], ms=[])
                return blocks[u]

            def logits(u):
                blk = block(u)
                q = blk["q"]
                q2 = jnp.concatenate([jnp.where(low, q, 0.0), jnp.where(low, 0.0, q)], axis=0).astype(BF16)
                sp = jnp.where(prev_ok2, _dot_nt(q2, blk["kp"]), NEG) + blk["no_prev"]
                sc = jnp.where(cur_ok2, _dot_nt(q2, blk["kc"]), NEG)
                return sp, sc

            def finish_chain(u, s):
                blk = block(u)
                m, p = _softmax_2blk(*s)
                pv = _dot(p, blk["vv"])
                dn = _dot(p, ones)
                q_rows = blk["q_rows"]
                num = jnp.where(low, pv[:BAND], pv[BAND:])
                den = jnp.where(low, dn[:BAND], dn[BAND:])
                m = jnp.where(low, m[:BAND], m[BAND:])
                if pi > 0:
                    m_old = m_ref[q_rows, :]
                    big = jnp.maximum(m_old, m)
                    a_old = jnp.exp(m_old - big)
                    a_new = jnp.exp(m - big)
                    num = a_old * num_ref[q_rows, :] + a_new * num
                    den = a_old * den_ref[q_rows, :] + a_new * den
                    m = big
                num_ref[q_rows, :] = num
                den_ref[q_rows, :] = den
                m_ref[q_rows, :] = m

            _software_pipeline(BLOCKS_PER_ITER, logits, finish_chain, BAND_DEPTH)
            return carry

        assert (r * nblk) % BLOCKS_PER_ITER == 0
        lax.fori_loop(0, r * nblk // BLOCKS_PER_ITER, body, 0)

    def finish(i, carry):
        sl = pl.ds(pl.multiple_of(i * BAND, BAND), BAND)
        o_ref[sl, :] = num_ref[sl, :] / den_ref[sl, :]
        return carry

    lax.fori_loop(0, seq // BAND, finish, 0)


def _prompt_a(qa, ka, va, n_seq, seq):
    n = qa.shape[0]
    blk = pl.BlockSpec((seq, LANES), lambda b, hp: (b, hp))
    return pl.pallas_call(
        _prompt_a_kernel,
        grid=(n_seq, A_W // LANES),
        in_specs=[blk, blk, blk],
        out_specs=blk,
        out_shape=jax.ShapeDtypeStruct((n, A_W), F32),
        scratch_shapes=[pltpu.VMEM((seq, LANES), F32)] * 3,
        compiler_params=_params("parallel", "parallel"),
        name="prompt_a",
    )(qa, ka, va)


def _prompt_b_kernel(sink_ref, q_ref, k_t_ref, v_t_ref, o_ref):
    seq = q_ref.shape[0]
    g = pl.program_id(1)
    ri = lax.broadcasted_iota(jnp.int32, (BAND, BAND), 0)
    ci = lax.broadcasted_iota(jnp.int32, (BAND, BAND), 1)
    prev_ok = ci >= ri
    cur_ok = ci <= ri

    def body(it, carry):
        blocks = {}

        def block(u):
            if u not in blocks:
                n = it * BLOCKS_PER_ITER + u
                cur = pl.ds(pl.multiple_of(n * BAND, BAND), BAND)
                prev = pl.ds(pl.multiple_of(jnp.maximum(n - 1, 0) * BAND, BAND), BAND)
                vv = jnp.concatenate([v_t_ref[:, prev], v_t_ref[:, cur]], axis=1).astype(BF16)
                blocks[u] = dict(
                    cur=cur, q=q_ref[cur, :], kc=k_t_ref[:, cur].astype(BF16), kp=k_t_ref[:, prev].astype(BF16),
                    v_and_ones=jnp.concatenate([vv, jnp.ones_like(vv)], axis=0),
                    no_prev=jnp.where(n > 0, 0.0, NEG), outs=[])
            return blocks[u]

        def logits(k):
            u, hh = divmod(k, G_B)
            blk = block(u)
            qh = blk["q"][:, hh * HEAD_DIM:(hh + 1) * HEAD_DIM].astype(BF16)
            sp = jnp.where(prev_ok, _dot(qh, blk["kp"]), NEG) + blk["no_prev"]
            sc = jnp.where(cur_ok, _dot(qh, blk["kc"]), NEG)
            return sp, sc

        def finish_chain(k, s):
            u, hh = divmod(k, G_B)
            blk = block(u)
            m, p = _softmax_2blk(*s)
            num_den = _dot_nt(p, blk["v_and_ones"])
            num, den = num_den[:, :HEAD_DIM], num_den[:, HEAD_DIM:]
            sink = sink_ref[g * G_B + hh]
            big = jnp.maximum(m, sink)
            a = jnp.exp(m - big)
            blk["outs"].append(num * a / (den * a + jnp.exp(sink - big)))
            if hh + 1 == G_B:
                o_ref[blk["cur"], :] = jnp.concatenate(blk["outs"], axis=1)

        _software_pipeline(BLOCKS_PER_ITER * G_B, logits, finish_chain, BAND_DEPTH)
        return carry

    assert (seq // BAND) % BLOCKS_PER_ITER == 0
    lax.fori_loop(0, seq // BAND // BLOCKS_PER_ITER, body, 0)


def _prompt_b(sink, qb, kb_t, vb_t, n_seq, seq):
    n = qb.shape[0]
    gw = G_B * HEAD_DIM
    q_spec = pl.BlockSpec((seq, gw), lambda b, g: (b, g))
    kv_spec = pl.BlockSpec((None, HEAD_DIM, seq), lambda b, g: (b, g, 0))
    return pl.pallas_call(
        _prompt_b_kernel,
        grid=(n_seq, N_KV_B),
        in_specs=[pl.BlockSpec(memory_space=pltpu.SMEM), q_spec, kv_spec, kv_spec],
        out_specs=q_spec,
        out_shape=jax.ShapeDtypeStruct((n, BQ_W), F32),
        compiler_params=_params("parallel", "parallel"),
        name="prompt_b",
    )(sink, qb, kb_t, vb_t)


def _shifted_buffer(cache, new):
    return jnp.concatenate([cache[:, new.shape[1]:], new], axis=1)


def _sample_a_kernel(q_ref, ck_ref, cv_ref, nk_ref, nv_ref, *rest):
    o_ref, ok_ref, ov_ref = rest[-3:]
    t_new, buf = q_ref.shape[0], ck_ref.shape[1]
    q = q_ref[...].astype(BF16)
    kc, vc = ck_ref[...], cv_ref[...]
    kn, vn = nk_ref[...], nv_ref[...]

    def count(d):
        c = jnp.zeros(d.shape, F32)
        for window, r in A_PATTERNS:
            assert r & (r - 1) == 0
            c = c + jnp.where(((d & (r - 1)) == 0) & (d <= window) & (d >= 0), 1.0, 0.0)
        return c

    d_c = buf + lax.broadcasted_iota(jnp.int32, (t_new, buf), 0) - lax.broadcasted_iota(jnp.int32, (t_new, buf), 1)
    d_n = lax.broadcasted_iota(jnp.int32, (t_new, t_new), 0) - lax.broadcasted_iota(jnp.int32, (t_new, t_new), 1)
    w_c, w_n = count(d_c), count(d_n)
    s_c = jnp.where(w_c > 0, _dot(q, kc.astype(BF16)), NEG)
    s_n = jnp.where(w_n > 0, _dot(q, kn.astype(BF16)), NEG)
    m = jnp.maximum(jnp.max(s_c, axis=1, keepdims=True), jnp.max(s_n, axis=1, keepdims=True))
    e_c = jnp.exp(s_c - m) * w_c
    e_n = jnp.exp(s_n - m) * w_n
    den = jnp.sum(e_c, axis=1, keepdims=True) + jnp.sum(e_n, axis=1, keepdims=True)
    num = _dot_nt(e_c.astype(BF16), vc.astype(BF16)) + _dot_nt(e_n.astype(BF16), vn.astype(BF16))
    o_ref[...] = num / den
    ok_ref[...] = _shifted_buffer(kc, kn)
    ov_ref[...] = _shifted_buffer(vc, vn)


def _sample_a(q, cache_k_t, cache_v_t, new_k_t, new_v_t, e, prev_out):
    n_e, n_b, n_h, _, buf = cache_k_t.shape
    t_new = q.shape[2]
    q_spec = pl.BlockSpec((None, None, t_new, HEAD_DIM), lambda b, h: (b, h, 0, 0))
    c_spec = pl.BlockSpec((None, None, None, HEAD_DIM, buf), lambda b, h: (e, b, h, 0, 0))
    n_spec = pl.BlockSpec((None, None, HEAD_DIM, t_new), lambda b, h: (b, h, 0, 0))
    in_specs = [q_spec, c_spec, c_spec, n_spec, n_spec]
    args = [q, cache_k_t, cache_v_t, new_k_t, new_v_t]
    aliases = {}
    if prev_out is not None:
        in_specs += [pl.BlockSpec(memory_space=pl.ANY)] * 2
        args += list(prev_out)
        aliases = {5: 1, 6: 2}
    return pl.pallas_call(
        _sample_a_kernel,
        grid=(n_b, n_h),
        in_specs=in_specs,
        out_specs=[q_spec, c_spec, c_spec],
        out_shape=[jax.ShapeDtypeStruct(q.shape, F32),
                   jax.ShapeDtypeStruct(cache_k_t.shape, F32),
                   jax.ShapeDtypeStruct(cache_v_t.shape, F32)],
        input_output_aliases=aliases,
        compiler_params=_params("parallel", "parallel"),
        name="sample_a",
    )(*args)


def _sample_b_kernel(sink_ref, q_ref, ck_ref, cv_ref, nk_ref, nv_ref, *rest):
    o_ref, ok_ref, ov_ref = rest[-3:]
    rows, buf, t_new = q_ref.shape[1], ck_ref.shape[2], nk_ref.shape[2]
    t_c = lax.broadcasted_iota(jnp.int32, (rows, buf), 0) & (t_new - 1)
    d_c = buf + t_c - lax.broadcasted_iota(jnp.int32, (rows, buf), 1)
    t_n = lax.broadcasted_iota(jnp.int32, (rows, t_new), 0) & (t_new - 1)
    d_n = t_n - lax.broadcasted_iota(jnp.int32, (rows, t_new), 1)
    ok_c = d_c <= WINDOW_B
    ok_n = (d_n >= 0) & (d_n <= WINDOW_B)
    head = lax.broadcasted_iota(jnp.int32, (rows, 1), 0) // t_new
    for g in range(N_KV_B):
        q = q_ref[g].astype(BF16)
        kc, vc = ck_ref[g], cv_ref[g]
        kn, vn = nk_ref[g], nv_ref[g]
        s_c = jnp.where(ok_c, _dot(q, kc.astype(BF16)), NEG)
        s_n = jnp.where(ok_n, _dot(q, kn.astype(BF16)), NEG)
        m = jnp.maximum(jnp.max(s_c, axis=1, keepdims=True), jnp.max(s_n, axis=1, keepdims=True))
        e_c = jnp.exp(s_c - m)
        e_n = jnp.exp(s_n - m)
        den = jnp.sum(e_c, axis=1, keepdims=True) + jnp.sum(e_n, axis=1, keepdims=True)
        num = _dot_nt(e_c.astype(BF16), vc.astype(BF16)) + _dot_nt(e_n.astype(BF16), vn.astype(BF16))
        sink = jnp.zeros((rows, 1), F32)
        for hh in range(G_B):
            sink = jnp.where(head == hh, sink_ref[g * G_B + hh], sink)
        big = jnp.maximum(m, sink)
        a = jnp.exp(m - big)
        o_ref[g] = num * a / (den * a + jnp.exp(sink - big))
        ok_ref[g] = _shifted_buffer(kc, kn)
        ov_ref[g] = _shifted_buffer(vc, vn)


def _sample_b(sink, q, cache_k_t, cache_v_t, new_k_t, new_v_t, e, prev_out):
    n_e, n_b, n_kv, _, buf = cache_k_t.shape
    rows, t_new = q.shape[2], new_k_t.shape[3]
    assert t_new & (t_new - 1) == 0
    q_spec = pl.BlockSpec((None, n_kv, rows, HEAD_DIM), lambda b: (b, 0, 0, 0))
    c_spec = pl.BlockSpec((None, None, n_kv, HEAD_DIM, buf), lambda b: (e, b, 0, 0, 0))
    n_spec = pl.BlockSpec((None, n_kv, HEAD_DIM, t_new), lambda b: (b, 0, 0, 0))
    in_specs = [pl.BlockSpec(memory_space=pltpu.SMEM), q_spec, c_spec, c_spec, n_spec, n_spec]
    args = [sink, q, cache_k_t, cache_v_t, new_k_t, new_v_t]
    aliases = {}
    if prev_out is not None:
        in_specs += [pl.BlockSpec(memory_space=pl.ANY)] * 2
        args += list(prev_out)
        aliases = {6: 1, 7: 2}
    return pl.pallas_call(
        _sample_b_kernel,
        grid=(n_b,),
        in_specs=in_specs,
        out_specs=[q_spec, c_spec, c_spec],
        out_shape=[jax.ShapeDtypeStruct(q.shape, F32),
                   jax.ShapeDtypeStruct(cache_k_t.shape, F32),
                   jax.ShapeDtypeStruct(cache_v_t.shape, F32)],
        input_output_aliases=aliases,
        compiler_params=_params("parallel"),
        name="sample_b",
    )(*args)


def _upper_ones(n):
    return jnp.where(lax.broadcasted_iota(jnp.int32, (n, n), 0) <= lax.broadcasted_iota(jnp.int32, (n, n), 1),
                     1.0, 0.0).astype(F32)


def _cumsum_lanes(x, with_total=False):
    rows, n = x.shape
    hi = x.astype(BF16)
    rest = x - hi.astype(F32)
    mid = rest.astype(BF16)
    lo = (rest - mid.astype(F32)).astype(BF16)
    rhs = _upper_ones(n).astype(BF16)
    if with_total:
        rhs = jnp.concatenate([rhs, jnp.ones((n, n), BF16)], axis=1)
    parts = _dot(jnp.concatenate([hi, mid, lo], axis=0), rhs)
    both = parts[0:rows] + parts[rows:2 * rows] + parts[2 * rows:3 * rows]
    return (both[:, :n], both[:, n:]) if with_total else both


def _cum_kernel(lf_ref, o_ref, carry_ref):
    @pl.when(pl.program_id(1) == 0)
    def _():
        carry_ref[...] = jnp.zeros_like(carry_ref)

    width = lf_ref.shape[1]
    c = _cumsum_lanes(lf_ref[...]) + carry_ref[:, 0:1]
    o_ref[...] = c * LOG2E
    carry_ref[...] = jnp.broadcast_to(c[:, width - 1:width], carry_ref.shape)


def _prompt_cum(lf_t, n_seq, seq):
    heads = lf_t.shape[1]
    spec = pl.BlockSpec((None, heads, CUM_BLOCK), lambda b, j: (b, 0, j))
    return pl.pallas_call(
        _cum_kernel,
        grid=(n_seq, seq // CUM_BLOCK),
        in_specs=[spec],
        out_specs=spec,
        out_shape=jax.ShapeDtypeStruct((n_seq, heads, seq), F32),
        scratch_shapes=[pltpu.VMEM((heads, LANES), F32)],
        compiler_params=_params("parallel", "arbitrary"),
        name="prompt_cum",
    )(lf_t)


def _fox_prompt_kernel(q_ref, k_t_ref, v_t_ref, cum_ref, o_ref, m_ref, acc_ref):
    i, j = pl.program_id(1), pl.program_id(2)
    tq, tk = q_ref.shape[0], k_t_ref.shape[1]

    @pl.when(j == 0)
    def _():
        m_ref[...] = jnp.full(m_ref.shape, NEG, F32)
        acc_ref[...] = jnp.zeros_like(acc_ref)

    def step(masked):
        def body(r, carry):
            rows = pl.ds(pl.multiple_of(r * FOX_SUB, FOX_SUB), FOX_SUB)
            if masked:
                q_pos = i * tq + r * FOX_SUB + lax.broadcasted_iota(jnp.int32, (FOX_SUB, tk), 0)
                causal = j * tk + lax.broadcasted_iota(jnp.int32, (FOX_SUB, tk), 1) <= q_pos
            kv = {}

            def group(g):
                if g not in kv:
                    v_t = v_t_ref[g * HEAD_DIM:(g + 1) * HEAD_DIM, :].astype(BF16)
                    kv[g] = (k_t_ref[g * HEAD_DIM:(g + 1) * HEAD_DIM, :].astype(BF16),
                             jnp.concatenate([v_t, jnp.ones_like(v_t)], axis=0))
                return kv[g]

            def logits(h):
                qh = q_ref[rows, h * HEAD_DIM:(h + 1) * HEAD_DIM].astype(BF16)
                return _dot(qh, group(h // G_C)[0])

            def finish_chain(h, s):
                s = s - cum_ref[h:h + 1, :]
                if masked:
                    s = jnp.where(causal, s, NEG)
                m_old = m_ref[h, rows, :]
                m_new = jnp.maximum(m_old, jnp.max(s, axis=1, keepdims=True))
                p = jnp.exp2(s - m_new).astype(BF16)
                acc_ref[h, rows, :] = jnp.exp2(m_old - m_new) * acc_ref[h, rows, :] + _dot_nt(p, group(h // G_C)[1])
                m_ref[h, rows, :] = m_new

            _software_pipeline(N_HEADS_C, logits, finish_chain, FOX_DEPTH)
            return carry

        lax.fori_loop(0, tq // FOX_SUB, body, 0)

    all_visible = (j + 1) * tk - 1 <= i * tq
    some_visible = j * tk <= i * tq + tq - 1
    pl.when(all_visible)(lambda: step(False))
    pl.when(jnp.logical_and(some_visible, jnp.logical_not(all_visible)))(lambda: step(True))

    @pl.when(j == pl.num_programs(2) - 1)
    def _():
        o_ref[...] = jnp.concatenate(
            [acc_ref[h][:, :HEAD_DIM] / acc_ref[h][:, HEAD_DIM:] for h in range(N_HEADS_C)], axis=1)


def _fox_prompt(q, k_t, v_t, cum_t, n_seq, seq):
    n = q.shape[0]
    nq, nk = seq // FOX_TQ, seq // FOX_TK

    def kv_idx(b, i, j):
        return (b, 0, jnp.minimum(j, (i * FOX_TQ + FOX_TQ - 1) // FOX_TK))

    q_spec = pl.BlockSpec((FOX_TQ, CQ_W), lambda b, i, j: (b * nq + i, 0))
    return pl.pallas_call(
        _fox_prompt_kernel,
        grid=(n_seq, nq, nk),
        in_specs=[q_spec,
                  pl.BlockSpec((None, CKV_W, FOX_TK), kv_idx),
                  pl.BlockSpec((None, CKV_W, FOX_TK), kv_idx),
                  pl.BlockSpec((None, N_HEADS_C, FOX_TK), kv_idx)],
        out_specs=q_spec,
        out_shape=jax.ShapeDtypeStruct((n, CQ_W), F32),
        scratch_shapes=[pltpu.VMEM((N_HEADS_C, FOX_TQ, 1), F32),
                        pltpu.VMEM((N_HEADS_C, FOX_TQ, 2 * HEAD_DIM), F32)],
        compiler_params=_params("parallel", "parallel", "arbitrary"),
        name="fox_prompt",
    )(q, k_t, v_t, cum_t)


def _fox_sample_kernel(pt_ref, q_ref, *refs, layer):
    (ck_hbm, cv_hbm, cl_hbm, nk_ref, nv_ref, nlf_ref, o_ref,
     k_buf, v_buf, lf_buf, sems, m_ref, l_ref, acc_ref, pref_ref) = refs
    npg = k_buf.shape[1]
    b, j, n_j = pl.program_id(0), pl.program_id(1), pl.num_programs(1)
    step = b * n_j + j
    slot = step % 2
    page = lf_buf.shape[3]
    t_new = nk_ref.shape[2]
    rows = N_HEADS_C * t_new

    def page_copies(seq_idx, chunk_idx, to_slot):
        copies = []
        for i in range(npg):
            pg = pt_ref[seq_idx, chunk_idx * npg + i]
            copies.append(pltpu.make_async_copy(ck_hbm.at[layer, pg], k_buf.at[to_slot, i], sems.at[to_slot, 0]))
            copies.append(pltpu.make_async_copy(cv_hbm.at[layer, pg], v_buf.at[to_slot, i], sems.at[to_slot, 1]))
            copies.append(pltpu.make_async_copy(cl_hbm.at[layer, pg], lf_buf.at[to_slot, i], sems.at[to_slot, 2]))
        return copies

    @pl.when(step == 0)
    def _():
        for cp in page_copies(b, j, slot):
            cp.start()

    nxt = step + 1

    @pl.when(nxt < pl.num_programs(0) * n_j)
    def _():
        for cp in page_copies(nxt // n_j, nxt % n_j, 1 - slot):
            cp.start()

    for cp in page_copies(b, j, slot):
        cp.wait()

    k_pages = [k_buf.at[slot, i] for i in range(npg)]
    v_pages = [v_buf.at[slot, i] for i in range(npg)]
    lf_pages = [lf_buf.at[slot, i] for i in range(npg)]

    @pl.when(j == 0)
    def _():
        m_ref[...] = jnp.full(m_ref.shape, NEG, F32)
        l_ref[...] = jnp.zeros_like(l_ref)
        acc_ref[...] = jnp.zeros_like(acc_ref)
        pref_ref[...] = jnp.zeros_like(pref_ref)

    def per_row(c):
        return jnp.concatenate([c] * t_new, axis=0)

    def stacked(ref):
        return ref[...].reshape(CKV_W, ref.shape[2]).astype(BF16)

    def update(s_parts, v_parts):
        m_old = m_ref[...]
        m_new = m_old
        for s in s_parts:
            m_new = jnp.maximum(m_new, jnp.max(s, axis=1, keepdims=True))
        alpha = jnp.exp2(m_old - m_new)
        l_new = alpha * l_ref[...]
        acc = alpha * acc_ref[...]
        for s, v_t in zip(s_parts, v_parts):
            p = jnp.exp2(s - m_new)
            l_new = l_new + jnp.sum(p, axis=1, keepdims=True)
            acc = acc + _dot_nt(p.astype(BF16), v_t)
        l_ref[...] = l_new
        acc_ref[...] = acc
        m_ref[...] = m_new

    def chunk(refs, c):
        return jnp.concatenate([stacked(r) for r in refs[c * FOX_PAGE_CHUNK:(c + 1) * FOX_PAGE_CHUNK]], axis=1)

    q = q_ref[...]
    n_chunks = npg // FOX_PAGE_CHUNK
    qk = [_dot(q, chunk(k_pages, c)) for c in range(n_chunks)]
    local, total = _cumsum_lanes(jnp.concatenate([r[...] for r in lf_pages], axis=0), with_total=True)
    pref = pref_ref[...]
    bias = []
    for i in range(npg):
        sl = slice(i * N_HEADS_C, (i + 1) * N_HEADS_C)
        bias.append(per_row((local[sl] + pref) * LOG2E))
        pref = pref + total[sl]
    pref_ref[...] = pref
    update([qk[c] - jnp.concatenate(bias[c * FOX_PAGE_CHUNK:(c + 1) * FOX_PAGE_CHUNK], axis=1)
            for c in range(n_chunks)], [chunk(v_pages, c) for c in range(n_chunks)])

    @pl.when(j == pl.num_programs(1) - 1)
    def _():
        lf = nlf_ref[...]
        lane = lax.broadcasted_iota(jnp.int32, lf.shape, 1)
        cum_n = pref[:, :t_new]
        for t in range(t_new):
            cum_n = cum_n + jnp.where(lane >= t, lf[:, t:t + 1], 0.0)
        tok = lax.broadcasted_iota(jnp.int32, (rows, t_new), 0) // N_HEADS_C
        causal = lax.broadcasted_iota(jnp.int32, (rows, t_new), 1) <= tok
        s = _dot(q, stacked(nk_ref)) - per_row(cum_n * LOG2E)
        update([jnp.where(causal, s, NEG)], [stacked(nv_ref)])
        out = acc_ref[...] / l_ref[...]
        kv_head = (lax.broadcasted_iota(jnp.int32, (rows, HEAD_DIM), 0) % N_HEADS_C) // G_C
        o = out[:, 0:HEAD_DIM]
        for g in range(1, N_KV_C):
            o = jnp.where(kv_head == g, out[:, g * HEAD_DIM:(g + 1) * HEAD_DIM], o)
        o_ref[...] = o


def _fox_sample(page_table, q, cache_k_t, cache_v_t, cache_lf_t, new_k_t, new_v_t, new_lf_t, o):
    n_b, rows, _ = q.shape
    page = cache_k_t.shape[-1]
    t_new = new_k_t.shape[-1]
    n_pages = page_table.shape[1]
    npg = PAGES_PER_STEP
    assert t_new & (t_new - 1) == 0 and n_pages % npg == 0

    q_spec = pl.BlockSpec((None, rows, CKV_W), lambda b, j, pt: (b, 0, 0))
    o_spec = pl.BlockSpec((None, rows, HEAD_DIM), lambda b, j, pt: (b, 0, 0))
    in_specs = [q_spec] + [pl.BlockSpec(memory_space=pl.ANY)] * 3
    in_specs += [pl.BlockSpec((None, N_KV_C, HEAD_DIM, t_new), lambda b, j, pt: (b, 0, 0, 0))] * 2
    in_specs += [pl.BlockSpec((None, N_HEADS_C, t_new), lambda b, j, pt: (b, 0, 0))]
    grid_spec = pltpu.PrefetchScalarGridSpec(
        num_scalar_prefetch=1,
        grid=(n_b, n_pages // npg),
        in_specs=in_specs,
        out_specs=o_spec,
        scratch_shapes=[pltpu.VMEM((2, npg, N_KV_C, HEAD_DIM, page), F32),
                        pltpu.VMEM((2, npg, N_KV_C, HEAD_DIM, page), F32),
                        pltpu.VMEM((2, npg, N_HEADS_C, page), F32),
                        pltpu.SemaphoreType.DMA((2, 3)),
                        pltpu.VMEM((rows, 1), F32), pltpu.VMEM((rows, 1), F32),
                        pltpu.VMEM((rows, CKV_W), F32), pltpu.VMEM((N_HEADS_C, LANES), F32)],
    )
    return pl.pallas_call(
        functools.partial(_fox_sample_kernel, layer=o),
        grid_spec=grid_spec,
        out_shape=jax.ShapeDtypeStruct((n_b, rows, HEAD_DIM), F32),
        compiler_params=_params("arbitrary", "arbitrary"),
        name="fox_sample",
    )(page_table, q, cache_k_t, cache_v_t, cache_lf_t, new_k_t, new_v_t, new_lf_t)


def _rope_tables(seq, n_sample_seq, t_new, past_len):
    inv_freq = ROPE_THETA ** (-jnp.arange(HALF, dtype=F32) / HALF)

    def tables(pos):
        ang = pos.astype(F32)[:, None] * inv_freq[None, :]
        return jnp.cos(ang), jnp.sin(ang)

    cos_p, sin_p = tables(jnp.arange(seq))
    cos_s, sin_s = tables(past_len + jnp.arange(t_new))
    return cos_p, sin_p, cos_s, sin_s


def kernel(x_prompt, x_sample, cache_a_k, cache_a_v, cache_b_k, cache_b_v, cache_c_k, cache_c_v, cache_c_logf, page_table, p_prompt, p_sample, g_ffn1, w_ffn1_in, w_ffn1_out, g_mix, w_in_ab, w_out_ab, sink_b, w_in_c, b_forget, w_out_c, g_ffn2, w_ffn2_in, w_ffn2_out, g_ple, w_ple_gate, w_ple_proj, g_final):
    n_seq, seq, d = x_prompt.shape
    n_dec, t_new, _ = x_sample.shape
    depth = g_ffn1.shape[0]
    n_p, n_s = n_seq * seq, n_dec * t_new
    n = n_p + n_s
    past_len = page_table.shape[1] * cache_c_k.shape[2]
    assert n_p % ROW_TILE == 0 and n_s % ROW_TILE == 0 and seq % ROW_TILE == 0

    x_in = [x_prompt.reshape(n_p, d), x_sample.reshape(n_s, d)]
    p_rows = (p_prompt.reshape(depth, n_p, -1), p_sample.reshape(depth, n_s, -1))

    cos_p, sin_p, cos_s, sin_s = _rope_tables(seq, n_dec, t_new, past_len)
    cos = jnp.concatenate([jnp.tile(cos_p, (n_seq, 1)), jnp.tile(cos_s, (n_dec, 1))], axis=0)
    sin = jnp.concatenate([jnp.tile(sin_p, (n_seq, 1)), jnp.tile(sin_s, (n_dec, 1))], axis=0)
    reps = LANES // HEAD_DIM
    tables = (jnp.tile(cos, (1, 2 * reps)), jnp.tile(jnp.concatenate([-sin, sin], axis=1), (1, reps)), cos.T, sin.T)

    bf = lambda w: w.astype(BF16)
    g3 = lambda g: g.reshape(depth, 1, d)
    g_ffn1, g_mix, g_ffn2, g_ple = g3(g_ffn1), g3(g_mix), g3(g_ffn2), g3(g_ple)
    w_ffn1_in, w_ffn1_out, w_ffn2_in, w_ffn2_out = bf(w_ffn1_in), bf(w_ffn1_out), bf(w_ffn2_in), bf(w_ffn2_out)
    w_ple_gate, w_ple_proj, w_out_ab, w_out_c = bf(w_ple_gate), bf(w_ple_proj), bf(w_out_ab), bf(w_out_c)
    qb_lo, kb_lo = 3 * A_W, 3 * A_W + BQ_W
    w_ab_rows = bf(w_in_ab[:, :, :kb_lo])
    w_ab_t = bf(jnp.swapaxes(jnp.concatenate([w_in_ab[:, :, A_W:3 * A_W], w_in_ab[:, :, kb_lo:]], axis=2), 1, 2))
    w_c_q = bf(w_in_c[:, :, :CQ_W])
    w_c_t = bf(jnp.swapaxes(w_in_c[:, :, CQ_W:], 1, 2))
    b_forget3 = b_forget.reshape(b_forget.shape[0], N_HEADS_C, 1)

    tr = lambda c: jnp.transpose(c, (0, 1, 3, 4, 2))
    cache_a_k_t, cache_a_v_t, cache_b_k_t, cache_b_v_t = tr(cache_a_k), tr(cache_a_v), tr(cache_b_k), tr(cache_b_v)
    cache_c_k_t, cache_c_v_t = tr(cache_c_k), tr(cache_c_v)
    cache_c_lf_t = jnp.transpose(cache_c_logf, (0, 1, 3, 2))

    assert n_s <= seq

    def sample_cols(a_t, heads):
        return a_t[n_seq, :, :n_s].reshape(heads, HEAD_DIM, n_dec, t_new).transpose(2, 0, 1, 3)

    def prompt_cols(a_t, heads, keep):
        a = a_t[:n_seq, :, seq - keep:].reshape(n_seq, heads, HEAD_DIM, keep)
        return a.transpose(0, 3, 1, 2)

    kv_of_head = jnp.repeat(jnp.eye(N_KV_C, dtype=F32), G_C, axis=0)

    buf_a, buf_b = cache_a_k.shape[2], cache_b_k.shape[2]
    keep_a, keep_b = min(buf_a, seq), min(buf_b, seq)
    ak_p, av_p, bk_p, bv_p = [], [], [], []
    ck_p, cv_p, cl_p, ck_s, cv_s, cl_s = [], [], [], [], [], []
    a_out, b_out = None, None

    for l in range(depth):
        x = _ffn(x_in if l == 0 else [x], g_ffn1, w_ffn1_in, w_ffn1_out, l)
        if l % 2 == 0:
            e = l // 2
            qa, ka, va, qb, ka_t, va_t, kb_t, vb_t = _proj_ab(x, g_mix, w_ab_rows, w_ab_t, tables, l, e, seq)
            oa = _prompt_a(qa, ka, va, n_seq, seq)
            ob = _prompt_b(sink_b[e], qb, kb_t, vb_t, n_seq, seq)
            qa_s = qa[n_p:].reshape(n_dec, t_new, N_HEADS_A, HEAD_DIM).transpose(0, 2, 1, 3)
            a_res = _sample_a(qa_s, cache_a_k_t, cache_a_v_t, sample_cols(ka_t, N_HEADS_A),
                              sample_cols(va_t, N_HEADS_A), e, a_out)
            oa_s, a_out = a_res[0], a_res[1:]
            qb_s = qb[n_p:].reshape(n_dec, t_new, N_KV_B, G_B, HEAD_DIM).transpose(0, 2, 3, 1, 4)
            qb_s = qb_s.reshape(n_dec, N_KV_B, G_B * t_new, HEAD_DIM)
            b_res = _sample_b(sink_b[e], qb_s, cache_b_k_t, cache_b_v_t, sample_cols(kb_t, N_KV_B),
                              sample_cols(vb_t, N_KV_B), e, b_out)
            ob_s, b_out = b_res[0], b_res[1:]
            oa_s = oa_s.transpose(0, 2, 1, 3).reshape(n_s, A_W)
            ob_s = ob_s.reshape(n_dec, N_KV_B, G_B, t_new, HEAD_DIM).transpose(0, 3, 1, 2, 4).reshape(n_s, BQ_W)
            oa = lax.dynamic_update_slice(oa, oa_s, (n_p, 0))
            ob = lax.dynamic_update_slice(ob, ob_s, (n_p, 0))
            x = _out_proj(x, [oa, ob], w_out_ab, e)
            ak_p.append(prompt_cols(ka_t, N_HEADS_A, keep_a))
            av_p.append(prompt_cols(va_t, N_HEADS_A, keep_a))
            bk_p.append(prompt_cols(kb_t, N_KV_B, keep_b))
            bv_p.append(prompt_cols(vb_t, N_KV_B, keep_b))
        else:
            o = l // 2
            qc, kc_t, vc_t, lf_t = _proj_c(x, g_mix, w_c_q, w_c_t, b_forget3, l, o, seq)
            cum_t = _prompt_cum(lf_t, n_seq, seq)
            oc = _fox_prompt(qc, kc_t, vc_t, cum_t, n_seq, seq)
            qc_s = qc[n_p:].reshape(n_dec, t_new, N_HEADS_C, 1, HEAD_DIM)
            qc_s = (qc_s * kv_of_head[None, None, :, :, None]).astype(BF16)
            qc_s = qc_s.reshape(n_dec, t_new * N_HEADS_C, CKV_W)
            lf_s = lf_t[n_seq, :, :n_s].reshape(N_HEADS_C, n_dec, t_new)
            new_lf_t = lf_s.transpose(1, 0, 2)
            oc_s = _fox_sample(page_table, qc_s, cache_c_k_t, cache_c_v_t, cache_c_lf_t,
                               sample_cols(kc_t, N_KV_C), sample_cols(vc_t, N_KV_C), new_lf_t, o)
            oc = lax.dynamic_update_slice(oc, oc_s.reshape(n_s, CQ_W), (n_p, 0))
            x = _out_proj(x, [oc], w_out_c, o)
            ck_p.append(prompt_cols(kc_t, N_KV_C, seq))
            cv_p.append(prompt_cols(vc_t, N_KV_C, seq))
            cl_p.append(lf_t[:n_seq].transpose(0, 2, 1))
            ck_s.append(sample_cols(kc_t, N_KV_C).transpose(0, 3, 1, 2))
            cv_s.append(sample_cols(vc_t, N_KV_C).transpose(0, 3, 1, 2))
            cl_s.append(lf_s.transpose(1, 2, 0))
        x = _ffn([x], g_ffn2, w_ffn2_in, w_ffn2_out, l)
        x = _ple(x, *p_rows, g_ple, w_ple_gate, w_ple_proj, l)

    y = _final_norm(x, g_final.reshape(1, d))
    back = lambda c: jnp.transpose(c, (0, 1, 4, 2, 3))
    return (y[:n_p].reshape(n_seq, seq, d), y[n_p:].reshape(n_dec, t_new, d),
            jnp.stack(ak_p), jnp.stack(av_p), back(a_out[0]), back(a_out[1]),
            jnp.stack(bk_p), jnp.stack(bv_p), back(b_out[0]), back(b_out[1]),
            jnp.stack(ck_p), jnp.stack(cv_p), jnp.stack(cl_p),
            jnp.stack(ck_s), jnp.stack(cv_s), jnp.stack(cl_s))
```
